```python
import math
import jax, jax.numpy as jnp
from jax import lax
import numpy as np

D_MODEL = 1024
BATCH = 8
SEQ = 2048
DEPTH = 2

P_DIM = 256
N_GROUPS = 4
GROUP_WIDTH = D_MODEL // N_GROUPS
HEAD_DIM = 64
N_HEADS = GROUP_WIDTH // HEAD_DIM
DIFF_QK_DIM = HEAD_DIM // 2
MOBA_BLOCK = 256
MOBA_TOPK = 3
Q_BLOCK = 128
GMLP_CHUNK = 128
CONV_WIDTH = 31
FFN_CONV_WIDTH = 3
D_FF = 256 * ((8 * D_MODEL // 3 + 255) // 256)
ROPE_THETA = 10000.0
EPS = 1e-6

kernel_name = "hymba_style_moba_gmlp_conformer_diffattn_block"


def rms_norm(x, g, eps=EPS):
    xf = x.astype(jnp.float32)
    y = xf * lax.rsqrt(jnp.mean(xf * xf, axis=-1, keepdims=True) + eps)
    return (y * g.astype(jnp.float32)).astype(x.dtype)


def layer_norm(x, g, b, eps=EPS):
    xf = x.astype(jnp.float32)
    mu = jnp.mean(xf, axis=-1, keepdims=True)
    xc = xf - mu
    y = xc * lax.rsqrt(jnp.mean(xc * xc, axis=-1, keepdims=True) + eps)
    return (y * g.astype(jnp.float32) + b.astype(jnp.float32)).astype(x.dtype)


def rope_tables(positions, dim):
    inv = ROPE_THETA ** (-(jnp.arange(0, dim, 2, dtype=jnp.float32) / dim))
    ang = positions.astype(jnp.float32)[..., None] * inv
    return jnp.cos(ang), jnp.sin(ang)


def apply_rope(x, cos, sin):
    c = cos[:, None].astype(x.dtype)
    s = sin[:, None].astype(x.dtype)
    x1, x2 = jnp.split(x, 2, axis=-1)
    return jnp.concatenate([x1 * c - x2 * s, x1 * s + x2 * c], axis=-1)


def causal_dwconv(x, w, b):
    K, C = w.shape
    y = lax.conv_general_dilated(
        x, w[:, None, :].astype(x.dtype), window_strides=(1,), padding=[(K - 1, 0)],
        dimension_numbers=("NWC", "WIO", "NWC"), feature_group_count=C)
    return y + b


def split_heads(t, n):
    B, S, _ = t.shape
    return t.reshape(B, S, n, -1).transpose(0, 2, 1, 3)


def merge_heads(t):
    B, H, S, d = t.shape
    return t.transpose(0, 2, 1, 3).reshape(B, S, H * d)


def moba_attention(q, k, v):
    B, H, S, Dh = q.shape
    nb = -(-S // MOBA_BLOCK)
    n_sel = min(MOBA_TOPK, nb)
    pad = nb * MOBA_BLOCK - S
    padw = ((0, 0), (0, 0), (0, pad), (0, 0))
    kb = jnp.pad(k, padw).reshape(B, H, nb, MOBA_BLOCK, Dh)
    vb = jnp.pad(v, padw).reshape(B, H, nb, MOBA_BLOCK, Dh)
    k_mean = jnp.mean(kb.astype(jnp.float32), axis=3)
    scale = Dh ** -0.5
    bi = jnp.arange(B)[:, None, None]
    hi = jnp.arange(H)[None, :, None]
    blk_ids = jnp.arange(nb)
    t_off = jnp.arange(Q_BLOCK)
    k_off = jnp.arange(MOBA_BLOCK)

    def query_block(qi):
        q0 = qi * Q_BLOCK
        cur = q0 // MOBA_BLOCK
        qc = lax.dynamic_slice_in_dim(q, q0, Q_BLOCK, axis=2)
        gate = jnp.einsum("bhtd,bhnd->bhtn", qc.astype(jnp.float32), k_mean)
        gate = jnp.where(blk_ids < cur, gate, -jnp.inf)
        _, sel = lax.top_k(gate, n_sel)
        sel_ok = sel < cur
        logits = []
        for s in range(n_sel):
            ks = kb[bi, hi, sel[..., s]]
            l = jnp.einsum("bhtd,bhtkd->bhtk", qc, ks).astype(jnp.float32) * scale
            logits.append(jnp.where(sel_ok[..., s, None], l, -jnp.inf))
        k_own = lax.dynamic_index_in_dim(kb, cur, axis=2, keepdims=False)
        l_own = jnp.einsum("bhtd,bhkd->bhtk", qc, k_own).astype(jnp.float32) * scale
        causal = (cur * MOBA_BLOCK + k_off)[None, :] <= (q0 + t_off)[:, None]
        logits.append(jnp.where(causal, l_own, -jnp.inf))
        probs = jax.nn.softmax(jnp.concatenate(logits, axis=-1), axis=-1).astype(v.dtype)
        probs = jnp.split(probs, n_sel + 1, axis=-1)
        v_own = lax.dynamic_index_in_dim(vb, cur, axis=2, keepdims=False)
        out = jnp.einsum("bhtk,bhkd->bhtd", probs[-1], v_own)
        for s in range(n_sel):
            out = out + jnp.einsum("bhtk,bhtkd->bhtd", probs[s], vb[bi, hi, sel[..., s]])
        return out

    out = lax.map(query_block, jnp.arange(S // Q_BLOCK))
    return out.transpose(1, 2, 0, 3, 4).reshape(B, H, S, Dh)


def diff_attention(q1, q2, k1, k2, v, lam):
    B, H, S, dc = q1.shape
    scale = dc ** -0.5
    kpos = jnp.arange(S)
    t_off = jnp.arange(Q_BLOCK)

    def query_block(qi):
        q0 = qi * Q_BLOCK
        causal = kpos[None, :] <= (q0 + t_off)[:, None]

        def attn_map(q, k):
            qc = lax.dynamic_slice_in_dim(q, q0, Q_BLOCK, axis=2)
            l = jnp.einsum("bhtd,bhsd->bhts", qc, k).astype(jnp.float32) * scale
            return jax.nn.softmax(jnp.where(causal, l, -jnp.inf), axis=-1)

        a = attn_map(q1, k1) - lam * attn_map(q2, k2)
        return jnp.einsum("bhts,bhsd->bhtd", a.astype(v.dtype), v)

    out = lax.map(query_block, jnp.arange(S // Q_BLOCK))
    return out.transpose(1, 2, 0, 3, 4).reshape(B, H, S, v.shape[-1])


def spatial_gating(z, ln_g, ln_b, ws, bs):
    B, S, _ = z.shape
    u, v = jnp.split(jax.nn.gelu(z), 2, axis=-1)
    v = layer_norm(v, ln_g, ln_b)
    nh, c = ws.shape[0], ws.shape[1]
    v = v.reshape(B, S // c, c, nh, -1)
    w = ws * jnp.tril(jnp.ones((c, c), ws.dtype))
    s = jnp.einsum("hij,bnjhd->bnihd", w, v) + bs.T[None, None, :, :, None]
    return u * s.reshape(B, S, -1)


def conformer_conv(z, dw_w, dw_b, ln_g, ln_b, pw_w, pw_b):
    a, g = jnp.split(z, 2, axis=-1)
    y = a * jax.nn.sigmoid(g)
    y = causal_dwconv(y, dw_w, dw_b)
    y = jax.nn.silu(layer_norm(y, ln_g, ln_b))
    return y @ pw_w + pw_b


def setup_inputs(seed: int = 0) -> dict:
    key = jax.random.key(seed)
    ks = iter(list(jax.random.split(key, 48)))
    L, D, GW = DEPTH, D_MODEL, GROUP_WIDTH

    def nrm(shape, scale):
        return scale * jax.random.normal(next(ks), shape, jnp.float32)

    def gain(shape):
        return 1.0 + nrm(shape, 0.02)

    offsets = jax.random.randint(next(ks), (BATCH, 1), 0, 1024, dtype=jnp.int32)
    positions = offsets + jnp.arange(SEQ, dtype=jnp.int32)[None, :]
    gmlp_bs = 1.0 + nrm((L, N_HEADS, GMLP_CHUNK), 0.01)
    return {
        "x": nrm((BATCH, SEQ, D), 1.0),
        "p": nrm((L, BATCH, SEQ, P_DIM), 1.0),
        "positions": positions,
        "pre_mix_norm": gain((L, D)),
        "w_in": nrm((L, D, 10 * GW), D ** -0.5),
        "gmlp_ln_g": gain((L, GW)),
        "gmlp_ln_b": nrm((L, GW), 0.02),
        "gmlp_ws": nrm((L, N_HEADS, GMLP_CHUNK, GMLP_CHUNK), GMLP_CHUNK ** -0.5),
        "gmlp_bs": gmlp_bs,
        "conv_dw_w": nrm((L, CONV_WIDTH, GW), CONV_WIDTH ** -0.5),
        "conv_dw_b": nrm((L, GW), 0.02),
        "conv_ln_g": gain((L, GW)),
        "conv_ln_b": nrm((L, GW), 0.02),
        "conv_pw_w": nrm((L, GW, GW), GW ** -0.5),
        "conv_pw_b": nrm((L, GW), 0.02),
        "diff_lq1": nrm((L, DIFF_QK_DIM), 0.1),
        "diff_lk1": nrm((L, DIFF_QK_DIM), 0.1),
        "diff_lq2": nrm((L, DIFF_QK_DIM), 0.1),
        "diff_lk2": nrm((L, DIFF_QK_DIM), 0.1),
        "diff_subln_g": gain((L, HEAD_DIM)),
        "out_norm_a": gain((L, GW)),
        "out_norm_b": gain((L, GW)),
        "out_norm_c": gain((L, GW)),
        "w_out": nrm((L, N_GROUPS * GW, D), (N_GROUPS * GW) ** -0.5),
        "post_mix_norm": gain((L, D)),
        "pre_ffn_norm": gain((L, D)),
        "w_up": nrm((L, D, 2 * D_FF), D ** -0.5),
        "ffn_conv_w": nrm((L, FFN_CONV_WIDTH, 2 * D_FF), FFN_CONV_WIDTH ** -0.5),
        "ffn_conv_b": nrm((L, 2 * D_FF), 0.02),
        "w_down": nrm((L, D_FF, D), D_FF ** -0.5),
        "post_ffn_norm": gain((L, D)),
        "pe_gate_norm": gain((L, D)),
        "w_pe_gate": nrm((L, D, D), D ** -0.5),
        "w_pe_proj": nrm((L, P_DIM, D), P_DIM ** -0.5),
    }


def reference(x, p, positions, pre_mix_norm, w_in, gmlp_ln_g, gmlp_ln_b, gmlp_ws, gmlp_bs,
              conv_dw_w, conv_dw_b, conv_ln_g, conv_ln_b, conv_pw_w, conv_pw_b,
              diff_lq1, diff_lk1, diff_lq2, diff_lk2, diff_subln_g,
              out_norm_a, out_norm_b, out_norm_c, w_out, post_mix_norm,
              pre_ffn_norm, w_up, ffn_conv_w, ffn_conv_b, w_down, post_ffn_norm,
              pe_gate_norm, w_pe_gate, w_pe_proj):
    GW = GROUP_WIDTH
    cos_a, sin_a = rope_tables(positions, HEAD_DIM)
    cos_d, sin_d = rope_tables(positions, DIFF_QK_DIM)
    for i in range(DEPTH):
        B, S, _ = x.shape
        h = rms_norm(x, pre_mix_norm[i])
        z = h @ w_in[i]
        z_a, z_b, z_c, z_d = jnp.split(z, [3 * GW, 5 * GW, 7 * GW], axis=-1)

        qa, ka, va = jnp.split(z_a, 3, axis=-1)
        qa = apply_rope(split_heads(qa, N_HEADS), cos_a, sin_a)
        ka = apply_rope(split_heads(ka, N_HEADS), cos_a, sin_a)
        out_a = rms_norm(merge_heads(moba_attention(qa, ka, split_heads(va, N_HEADS))), out_norm_a[i])

        out_b = rms_norm(spatial_gating(z_b, gmlp_ln_g[i], gmlp_ln_b[i], gmlp_ws[i], gmlp_bs[i]), out_norm_b[i])

        out_c = rms_norm(conformer_conv(z_c, conv_dw_w[i], conv_dw_b[i], conv_ln_g[i], conv_ln_b[i],
                                        conv_pw_w[i], conv_pw_b[i]), out_norm_c[i])

        qd, kd, vd = jnp.split(z_d, 3, axis=-1)
        qd = qd.reshape(B, S, N_HEADS, 2, DIFF_QK_DIM).transpose(0, 2, 1, 3, 4)
        kd = kd.reshape(B, S, N_HEADS, 2, DIFF_QK_DIM).transpose(0, 2, 1, 3, 4)
        q1 = apply_rope(qd[..., 0, :], cos_d, sin_d)
        q2 = apply_rope(qd[..., 1, :], cos_d, sin_d)
        k1 = apply_rope(kd[..., 0, :], cos_d, sin_d)
        k2 = apply_rope(kd[..., 1, :], cos_d, sin_d)
        lam_init = 0.8 - 0.6 * math.exp(-0.3 * i)
        lam = (jnp.exp(jnp.sum(diff_lq1[i].astype(jnp.float32) * diff_lk1[i].astype(jnp.float32)))
               - jnp.exp(jnp.sum(diff_lq2[i].astype(jnp.float32) * diff_lk2[i].astype(jnp.float32)))
               + lam_init)
        od = diff_attention(q1, q2, k1, k2, split_heads(vd, N_HEADS), lam)
        od = rms_norm(od, diff_subln_g[i], eps=1e-5) * (1.0 - lam_init)
        out_d = merge_heads(od)

        y = jnp.concatenate([out_a, out_b, out_c, out_d], axis=-1) @ w_out[i]
        x = x + rms_norm(y, post_mix_norm[i])

        u = causal_dwconv(rms_norm(x, pre_ffn_norm[i]) @ w_up[i], ffn_conv_w[i], ffn_conv_b[i])
        g, val = jnp.split(u, 2, axis=-1)
        f = (jax.nn.gelu(g) * val) @ w_down[i]
        x = x + rms_norm(f, post_ffn_norm[i])

        gate = jax.nn.sigmoid(rms_norm(x, pe_gate_norm[i]) @ w_pe_gate[i])
        x = x + gate * (p[i] @ w_pe_proj[i])
    return x
```

```python
import functools
import math

import jax
import jax.numpy as jnp
from jax import lax
from jax.experimental import pallas as pl
from jax.experimental.pallas import tpu as pltpu

F32 = jnp.float32
BF16 = jnp.bfloat16

D_MODEL = 1024
P_DIM = 256
GW = 256
HEAD_DIM = 64
N_HEADS = 4
DIFF_QK_DIM = 32
MOBA_BLOCK = 256
MOBA_TOPK = 3
GMLP_CHUNK = 128
CONV_WIDTH = 31
D_FF = 2816
ROPE_THETA = 10000.0
EPS = 1e-6
LANES = 128
NEG_INF = float("-inf")

TM_PROJ = 512
TQ = 256
TK = 256
T_BC = 256
CONV_HALO = 32
FFN_HALO = 8
FF_CHUNK = 256
VMEM_LIMIT = 56 * 1024 * 1024


def _cparams(n_axes):
    return pltpu.CompilerParams(dimension_semantics=("arbitrary",) * n_axes,
                                vmem_limit_bytes=VMEM_LIMIT)


def _rms(xf, g, eps=EPS):
    return xf * lax.rsqrt(jnp.mean(xf * xf, axis=-1, keepdims=True) + eps) * g


def _ln(xf, g, b, eps=EPS):
    mu = jnp.mean(xf, axis=-1, keepdims=True)
    xc = xf - mu
    return xc * lax.rsqrt(jnp.mean(xc * xc, axis=-1, keepdims=True) + eps) * g + b


def _dot_nt(a, b):
    return lax.dot_general(a, b, (((1,), (1,)), ((), ())), preferred_element_type=F32)


def _rope_tables_kernel(pos_ref, cosa_ref, sina_ref, cosd_ref, sind_ref):
    pos = pos_ref[...].astype(F32)
    lane = lax.broadcasted_iota(jnp.int32, (1, LANES), 1)

    def tables(dim):
        half = dim // 2
        idx = (lane % half).astype(F32)
        inv = jnp.exp(idx * (-2.0 / dim * math.log(ROPE_THETA)))
        ang = pos * inv
        sign = jnp.where((lane % dim) < half, -1.0, 1.0)
        return jnp.cos(ang), jnp.sin(ang) * sign

    cosa_ref[...], sina_ref[...] = tables(HEAD_DIM)
    cosd_ref[...], sind_ref[...] = tables(DIFF_QK_DIM)


def _rope_tables(pos2d):
    n = pos2d.shape[0]
    tm = 2048
    tab = jax.ShapeDtypeStruct((n, LANES), F32)
    spec = pl.BlockSpec((tm, LANES), lambda i: (i, 0))
    return pl.pallas_call(
        _rope_tables_kernel,
        grid=(n // tm,),
        in_specs=[pl.BlockSpec((tm, 1), lambda i: (i, 0))],
        out_specs=[spec] * 4,
        out_shape=[tab] * 4,
        compiler_params=_cparams(1),
        name="rope_tables",
    )(pos2d)


def _rope_slab(slab, cos, sin_signed, half):
    lane = lax.broadcasted_iota(jnp.int32, slab.shape, 1)
    first = (lane % (2 * half)) < half
    rot = jnp.where(first, pltpu.roll(slab, LANES - half, 1), pltpu.roll(slab, half, 1))
    return slab * cos + rot * sin_signed


def _inproj_kernel(x_ref, g_ref, w_ref, cosa_ref, sina_ref, cosd_ref, sind_ref, z_ref):
    h = _rms(x_ref[...], g_ref[...]).astype(BF16)
    rope = {0: (HEAD_DIM // 2, cosa_ref, sina_ref, HEAD_DIM ** -0.5),
            1: (HEAD_DIM // 2, cosa_ref, sina_ref, None),
            7: (DIFF_QK_DIM // 2, cosd_ref, sind_ref, None),
            8: (DIFF_QK_DIM // 2, cosd_ref, sind_ref, None)}
    for j in range(10):
        zj = jnp.dot(h, w_ref[:, j * GW:(j + 1) * GW], preferred_element_type=F32)
        if j in rope:
            half, cos_ref, sin_ref, scale = rope[j]
            for c in range(GW // LANES):
                slab = _rope_slab(zj[:, c * LANES:(c + 1) * LANES], cos_ref[...], sin_ref[...], half)
                if scale is not None:
                    slab = slab * scale
                z_ref[:, j * GW + c * LANES:j * GW + (c + 1) * LANES] = slab.astype(BF16)
        else:
            z_ref[:, j * GW:(j + 1) * GW] = zj.astype(BF16)


def _inproj(x2d, gain, w_bf16, tabs):
    n = x2d.shape[0]
    tm = TM_PROJ
    tab_spec = pl.BlockSpec((tm, LANES), lambda i: (i, 0))
    return pl.pallas_call(
        _inproj_kernel,
        grid=(n // tm,),
        in_specs=[pl.BlockSpec((tm, D_MODEL), lambda i: (i, 0)),
                  pl.BlockSpec((1, D_MODEL), lambda i: (0, 0)),
                  pl.BlockSpec((D_MODEL, 10 * GW), lambda i: (0, 0)),
                  tab_spec, tab_spec, tab_spec, tab_spec],
        out_specs=pl.BlockSpec((tm, 10 * GW), lambda i: (i, 0)),
        out_shape=jax.ShapeDtypeStruct((n, 10 * GW), BF16),
        compiler_params=_cparams(1),
        name="in_proj",
    )(x2d, gain, w_bf16, *tabs)


def _softmax_first(s, v_blk, st, idx):
    m_ref, l_ref, acc_ref = st
    m = jnp.max(s, axis=-1, keepdims=True)
    p = jnp.exp(s - m)
    m_ref[idx] = m
    l_ref[idx] = jnp.sum(p, axis=-1, keepdims=True)
    acc_ref[idx] = jnp.dot(p.astype(BF16), v_blk, preferred_element_type=F32)


def _softmax_update(s, v_blk, st, idx):
    m_ref, l_ref, acc_ref = st
    m_old = m_ref[idx]
    m_new = jnp.maximum(m_old, jnp.max(s, axis=-1, keepdims=True))
    alpha = jnp.exp(m_old - m_new)
    p = jnp.exp(s - m_new)
    m_ref[idx] = m_new
    l_ref[idx] = alpha * l_ref[idx] + jnp.sum(p, axis=-1, keepdims=True)
    acc_ref[idx] = alpha * acc_ref[idx] + jnp.dot(p.astype(BF16), v_blk, preferred_element_type=F32)


def _causal_mask(shape):
    row = lax.broadcasted_iota(jnp.int32, shape, 0)
    col = lax.broadcasted_iota(jnp.int32, shape, 1)
    return col <= row


def _moba_kernel(q_ref, k_ref, v_ref, gn_ref, o_ref, km_ref, fl_ref, m_ref, l_ref, acc_ref):
    qi = pl.program_id(1)
    n_blocks = k_ref.shape[0] // MOBA_BLOCK
    st = (m_ref, l_ref, acc_ref)

    @pl.when(qi == 0)
    def _():
        km_ref[...] = jnp.zeros_like(km_ref)
        lane = lax.broadcasted_iota(jnp.int32, (1, GW), 1)
        for n in range(n_blocks):
            kb = k_ref[n * MOBA_BLOCK:(n + 1) * MOBA_BLOCK, :].astype(F32)
            mean = jnp.sum(kb, axis=0, keepdims=True) * (1.0 / MOBA_BLOCK)
            for h in range(N_HEADS):
                km_ref[h * 8 + n:h * 8 + n + 1, :] = jnp.where(lane // HEAD_DIM == h, mean, 0.0)

    q_all = q_ref[...]
    g_t = lax.dot_general(km_ref[...], q_all.astype(F32), (((1,), (1,)), ((), ())),
                          precision=lax.Precision.HIGHEST, preferred_element_type=F32)
    blk = lax.broadcasted_iota(jnp.int32, (8, TQ), 0)
    n_past = jnp.full((8, TQ), qi, jnp.int32)
    rows = []
    for h in range(N_HEADS):
        g = g_t[h * 8:(h + 1) * 8, :]
        rank = jnp.zeros((8, TQ), F32)
        for m in range(n_blocks):
            gm = g[m:m + 1, :]
            beats = (gm > g) | ((gm == g) & (m < blk))
            rank = rank + jnp.where(beats & (m < n_past), 1.0, 0.0)
        rows.append(jnp.where((rank < MOBA_TOPK) & (blk < n_past), 1.0, 0.0))
    rows.append(jnp.zeros((LANES - 8 * N_HEADS, TQ), F32))
    fl_ref[...] = jnp.concatenate(rows, axis=0).T

    lane = lax.broadcasted_iota(jnp.int32, (TQ, LANES), 1)
    q_heads = []
    for h in range(N_HEADS):
        pair = q_all[:, (h // 2) * LANES:(h // 2 + 1) * LANES]
        q_heads.append(jnp.where((lane // HEAD_DIM) == (h % 2), pair, jnp.zeros_like(pair)))

    own = pl.multiple_of(qi * MOBA_BLOCK, MOBA_BLOCK)
    causal = _causal_mask((TQ, TK))
    for h in range(N_HEADS):
        cols = slice((h // 2) * LANES, (h // 2 + 1) * LANES)
        s = _dot_nt(q_heads[h], k_ref[pl.ds(own, TK), cols])
        _softmax_first(jnp.where(causal, s, NEG_INF), v_ref[pl.ds(own, TK), cols], st, h)

    def past_block(n, carry):
        start = pl.multiple_of(n * MOBA_BLOCK, MOBA_BLOCK)
        flags = pltpu.roll(fl_ref[...], lax.rem(LANES - n, LANES), 1)
        for h in range(N_HEADS):
            cols = slice((h // 2) * LANES, (h // 2 + 1) * LANES)
            s = _dot_nt(q_heads[h], k_ref[pl.ds(start, TK), cols])
            s = jnp.where(flags[:, h * 8:h * 8 + 1] > 0.5, s, NEG_INF)
            _softmax_update(s, v_ref[pl.ds(start, TK), cols], st, h)
        return carry

    lax.fori_loop(0, qi, past_block, 0)

    pairs = []
    for p in range(N_HEADS // 2):
        o_lo = acc_ref[2 * p] / l_ref[2 * p]
        o_hi = acc_ref[2 * p + 1] / l_ref[2 * p + 1]
        pairs.append(jnp.where(lane < HEAD_DIM, o_lo, o_hi))
    o_ref[...] = _rms(jnp.concatenate(pairs, axis=1), gn_ref[...]).astype(o_ref.dtype)


def _moba(z, gain, batch, seq):
    n = z.shape[0]
    nq = seq // TQ
    return pl.pallas_call(
        _moba_kernel,
        grid=(batch, nq),
        in_specs=[pl.BlockSpec((TQ, GW), lambda b, i: (b * nq + i, 0)),
                  pl.BlockSpec((seq, GW), lambda b, i: (b, 1)),
                  pl.BlockSpec((seq, GW), lambda b, i: (b, 2)),
                  pl.BlockSpec((1, GW), lambda b, i: (0, 0))],
        out_specs=pl.BlockSpec((TQ, GW), lambda b, i: (b * nq + i, 0)),
        out_shape=jax.ShapeDtypeStruct((n, GW), BF16),
        scratch_shapes=[pltpu.VMEM((LANES, GW), F32),
                        pltpu.VMEM((TQ, LANES), F32),
                        pltpu.VMEM((N_HEADS, TQ, 1), F32),
                        pltpu.VMEM((N_HEADS, TQ, 1), F32),
                        pltpu.VMEM((N_HEADS, TQ, LANES), F32)],
        compiler_params=_cparams(2),
        name="moba",
    )(z, z, z, gain)


def _diff_kernel(lam_init, q_ref, k_ref, v_ref, lvec_ref, gs_ref, o_ref, m_ref, l_ref, acc_ref):
    qi = pl.program_id(1)
    st = (m_ref, l_ref, acc_ref)
    scale = DIFF_QK_DIM ** -0.5
    n_maps = 2 * N_HEADS

    lv = lvec_ref[...]
    lam = (jnp.exp(jnp.sum(lv[0:1] * lv[1:2], axis=-1, keepdims=True))
           - jnp.exp(jnp.sum(lv[2:3] * lv[3:4], axis=-1, keepdims=True)) + lam_init)

    lane = lax.broadcasted_iota(jnp.int32, (TQ, LANES), 1)
    q_all = q_ref[...]
    q_maps = []
    for j in range(n_maps):
        pair = q_all[:, (j // 4) * LANES:(j // 4 + 1) * LANES]
        q_maps.append(jnp.where((lane // DIFF_QK_DIM) == (j % 4), pair, jnp.zeros_like(pair)))

    own = pl.multiple_of(qi * TK, TK)
    causal = _causal_mask((TQ, TK))
    for j in range(n_maps):
        cols = slice((j // 4) * LANES, (j // 4 + 1) * LANES)
        s = _dot_nt(q_maps[j], k_ref[pl.ds(own, TK), cols]) * scale
        _softmax_first(jnp.where(causal, s, NEG_INF), v_ref[pl.ds(own, TK), cols], st, j)

    def past_block(n, carry):
        start = pl.multiple_of(n * TK, TK)
        for j in range(n_maps):
            cols = slice((j // 4) * LANES, (j // 4 + 1) * LANES)
            s = _dot_nt(q_maps[j], k_ref[pl.ds(start, TK), cols]) * scale
            _softmax_update(s, v_ref[pl.ds(start, TK), cols], st, j)
        return carry

    lax.fori_loop(0, qi, past_block, 0)

    is_lo = lane < HEAD_DIM
    for p in range(N_HEADS // 2):
        heads = []
        for h in (2 * p, 2 * p + 1):
            a1 = acc_ref[2 * h] / l_ref[2 * h]
            a2 = acc_ref[2 * h + 1] / l_ref[2 * h + 1]
            heads.append(a1 - lam * a2)
        o = jnp.where(is_lo, heads[0], heads[1])
        sq = o * o
        s_lo = jnp.sum(jnp.where(is_lo, sq, 0.0), axis=-1, keepdims=True)
        s_hi = jnp.sum(jnp.where(is_lo, 0.0, sq), axis=-1, keepdims=True)
        ms = jnp.where(is_lo, s_lo, s_hi) * (1.0 / HEAD_DIM)
        y = o * lax.rsqrt(ms + 1e-5) * gs_ref[...] * (1.0 - lam_init)
        o_ref[:, p * LANES:(p + 1) * LANES] = y.astype(o_ref.dtype)


def _diff(z, lvec, gs, lam_init, batch, seq):
    n = z.shape[0]
    nq = seq // TQ
    return pl.pallas_call(
        functools.partial(_diff_kernel, lam_init),
        grid=(batch, nq),
        in_specs=[pl.BlockSpec((TQ, GW), lambda b, i: (b * nq + i, 7)),
                  pl.BlockSpec((seq, GW), lambda b, i: (b, 8)),
                  pl.BlockSpec((seq, GW), lambda b, i: (b, 9)),
                  pl.BlockSpec((4, DIFF_QK_DIM), lambda b, i: (0, 0)),
                  pl.BlockSpec((1, LANES), lambda b, i: (0, 0))],
        out_specs=pl.BlockSpec((TQ, GW), lambda b, i: (b * nq + i, 0)),
        out_shape=jax.ShapeDtypeStruct((n, GW), BF16),
        scratch_shapes=[pltpu.VMEM((2 * N_HEADS, TQ, 1), F32),
                        pltpu.VMEM((2 * N_HEADS, TQ, 1), F32),
                        pltpu.VMEM((2 * N_HEADS, TQ, LANES), F32)],
        compiler_params=_cparams(2),
        name="diff_attn",
    )(z, z, z, lvec, gs)


def _gmlp_conv_kernel(zu_ref, zv_ref, za_ref, zg_ref,
                      lng_ref, lnb_ref, ws_ref, bsf_ref, gnb_ref,
                      dww_ref, dwb_ref, clg_ref, clb_ref, pww_ref, pwb_ref, gnc_ref,
                      ob_ref, oc_ref, ybuf_ref):
    ti = pl.program_id(1)
    t = zu_ref.shape[0]

    u = jax.nn.gelu(zu_ref[...].astype(F32))
    v = _ln(jax.nn.gelu(zv_ref[...].astype(F32)), lng_ref[...], lnb_ref[...])
    row = lax.broadcasted_iota(jnp.int32, ws_ref.shape, 0)
    col = lax.broadcasted_iota(jnp.int32, ws_ref.shape, 1)
    w = jnp.where(col <= (row % GMLP_CHUNK), ws_ref[...], 0.0).astype(BF16)
    lane = lax.broadcasted_iota(jnp.int32, (GMLP_CHUNK, GW), 1)
    for c in range(t // GMLP_CHUNK):
        rows = slice(c * GMLP_CHUNK, (c + 1) * GMLP_CHUNK)
        r = jnp.dot(w, v[rows].astype(BF16), preferred_element_type=F32)
        s = bsf_ref[...]
        for h in range(N_HEADS):
            s = s + jnp.where(lane // HEAD_DIM == h, r[h * GMLP_CHUNK:(h + 1) * GMLP_CHUNK], 0.0)
        ob_ref[rows, :] = _rms(u[rows] * s, gnb_ref[...]).astype(ob_ref.dtype)

    @pl.when(ti == 0)
    def _():
        ybuf_ref[0:CONV_HALO, :] = jnp.zeros((CONV_HALO, GW), F32)

    @pl.when(ti > 0)
    def _():
        ybuf_ref[0:CONV_HALO, :] = ybuf_ref[t:t + CONV_HALO, :]

    ybuf_ref[CONV_HALO:CONV_HALO + t, :] = za_ref[...].astype(F32) * jax.nn.sigmoid(zg_ref[...].astype(F32))
    acc = jnp.zeros((t, GW), F32) + dwb_ref[...]
    first = CONV_HALO - (CONV_WIDTH - 1)
    for k in range(CONV_WIDTH):
        acc = acc + dww_ref[k:k + 1, :] * ybuf_ref[first + k:first + k + t, :]
    y = jax.nn.silu(_ln(acc, clg_ref[...], clb_ref[...]))
    y = jnp.dot(y.astype(BF16), pww_ref[...], preferred_element_type=F32) + pwb_ref[...]
    oc_ref[...] = _rms(y, gnc_ref[...]).astype(oc_ref.dtype)


def _gmlp_conv(z, prm, batch, seq):
    n = z.shape[0]
    nt = seq // T_BC
    zspec = lambda c: pl.BlockSpec((T_BC, GW), lambda b, i: (b * nt + i, c))
    const = lambda shape: pl.BlockSpec(shape, lambda b, i: (0, 0))
    out = jax.ShapeDtypeStruct((n, GW), BF16)
    return pl.pallas_call(
        _gmlp_conv_kernel,
        grid=(batch, nt),
        in_specs=[zspec(3), zspec(4), zspec(5), zspec(6),
                  const((1, GW)), const((1, GW)), const((N_HEADS * GMLP_CHUNK, GMLP_CHUNK)),
                  const((GMLP_CHUNK, GW)), const((1, GW)),
                  const((CONV_WIDTH, GW)), const((1, GW)), const((1, GW)), const((1, GW)),
                  const((GW, GW)), const((1, GW)), const((1, GW))],
        out_specs=[pl.BlockSpec((T_BC, GW), lambda b, i: (b * nt + i, 0))] * 2,
        out_shape=[out, out],
        scratch_shapes=[pltpu.VMEM((CONV_HALO + T_BC, GW), F32)],
        compiler_params=_cparams(2),
        name="gmlp_conv",
    )(z, z, z, z, *prm)


def _outproj_kernel(oa_ref, ob_ref, oc_ref, od_ref, w_ref, x_ref, g_ref, xo_ref):
    y = jnp.dot(oa_ref[...], w_ref[0:GW, :], preferred_element_type=F32)
    for j, o_ref in enumerate((ob_ref, oc_ref, od_ref), start=1):
        y = y + jnp.dot(o_ref[...], w_ref[j * GW:(j + 1) * GW, :], preferred_element_type=F32)
    xo_ref[...] = x_ref[...] + _rms(y, g_ref[...])


def _outproj(oa, ob, oc, od, w_bf16, x2d, gain):
    n = x2d.shape[0]
    tm = TM_PROJ
    ospec = pl.BlockSpec((tm, GW), lambda i: (i, 0))
    xspec = pl.BlockSpec((tm, D_MODEL), lambda i: (i, 0))
    return pl.pallas_call(
        _outproj_kernel,
        grid=(n // tm,),
        in_specs=[ospec, ospec, ospec, ospec,
                  pl.BlockSpec((4 * GW, D_MODEL), lambda i: (0, 0)),
                  xspec, pl.BlockSpec((1, D_MODEL), lambda i: (0, 0))],
        out_specs=xspec,
        out_shape=jax.ShapeDtypeStruct((n, D_MODEL), F32),
        compiler_params=_cparams(1),
        name="out_proj",
    )(oa, ob, oc, od, w_bf16, x2d, gain)


def _ffn_kernel(x_ref, gpre_ref, wup_ref, cw_ref, cb_ref, wdn_ref, gpost_ref, xo_ref,
                carry_ref, ubuf_ref, acc_ref):
    ti = pl.program_id(1)
    tm = x_ref.shape[0]

    @pl.when(ti == 0)
    def _():
        carry_ref[...] = jnp.zeros_like(carry_ref)

    x = x_ref[...]
    h = _rms(x, gpre_ref[...]).astype(BF16)

    def conv_cols(c0):
        cols = slice(c0, c0 + FF_CHUNK)
        u = jnp.dot(h, wup_ref[:, cols], preferred_element_type=F32)
        ubuf_ref[0:FFN_HALO, :] = carry_ref[:, cols]
        ubuf_ref[FFN_HALO:FFN_HALO + tm, :] = u
        carry_ref[:, cols] = u[tm - FFN_HALO:tm, :]
        return (cw_ref[0:1, cols] * ubuf_ref[FFN_HALO - 2:FFN_HALO - 2 + tm, :]
                + cw_ref[1:2, cols] * ubuf_ref[FFN_HALO - 1:FFN_HALO - 1 + tm, :]
                + cw_ref[2:3, cols] * u + cb_ref[:, cols])

    for j in range(D_FF // FF_CHUNK):
        gate = conv_cols(j * FF_CHUNK)
        val = conv_cols(D_FF + j * FF_CHUNK)
        act = (jax.nn.gelu(gate) * val).astype(BF16)
        part = jnp.dot(act, wdn_ref[j * FF_CHUNK:(j + 1) * FF_CHUNK, :], preferred_element_type=F32)
        if j == 0:
            acc_ref[...] = part
        else:
            acc_ref[...] += part
    xo_ref[...] = x + _rms(acc_ref[...], gpost_ref[...])


def _ffn(x2d, gpre, wup_bf16, cw, cb, wdn_bf16, gpost, batch, seq):
    n = x2d.shape[0]
    tm = TM_PROJ
    nt = seq // tm
    xspec = pl.BlockSpec((tm, D_MODEL), lambda b, i: (b * nt + i, 0))
    const = lambda shape: pl.BlockSpec(shape, lambda b, i: (0, 0), pipeline_mode=pl.Buffered(1))
    return pl.pallas_call(
        _ffn_kernel,
        grid=(batch, nt),
        in_specs=[xspec, const((1, D_MODEL)), const((D_MODEL, 2 * D_FF)),
                  const((3, 2 * D_FF)), const((1, 2 * D_FF)),
                  const((D_FF, D_MODEL)), const((1, D_MODEL))],
        out_specs=xspec,
        out_shape=jax.ShapeDtypeStruct((n, D_MODEL), F32),
        scratch_shapes=[pltpu.VMEM((FFN_HALO, 2 * D_FF), F32),
                        pltpu.VMEM((FFN_HALO + tm, FF_CHUNK), F32),
                        pltpu.VMEM((tm, D_MODEL), F32)],
        compiler_params=_cparams(2),
        name="ffn",
    )(x2d, gpre, wup_bf16, cw, cb, wdn_bf16, gpost)


def _pe_kernel(x_ref, p_ref, g_ref, wg_ref, wp_ref, xo_ref):
    x = x_ref[...]
    gate = jax.nn.sigmoid(jnp.dot(_rms(x, g_ref[...]).astype(BF16), wg_ref[...],
                                  preferred_element_type=F32))
    proj = jnp.dot(p_ref[...].astype(BF16), wp_ref[...], preferred_element_type=F32)
    xo_ref[...] = x + gate * proj


def _pe(x2d, p2d, gain, wg_bf16, wp_bf16):
    n = x2d.shape[0]
    tm = TM_PROJ
    xspec = pl.BlockSpec((tm, D_MODEL), lambda i: (i, 0))
    return pl.pallas_call(
        _pe_kernel,
        grid=(n // tm,),
        in_specs=[xspec, pl.BlockSpec((tm, P_DIM), lambda i: (i, 0)),
                  pl.BlockSpec((1, D_MODEL), lambda i: (0, 0)),
                  pl.BlockSpec((D_MODEL, D_MODEL), lambda i: (0, 0)),
                  pl.BlockSpec((P_DIM, D_MODEL), lambda i: (0, 0))],
        out_specs=xspec,
        out_shape=jax.ShapeDtypeStruct((n, D_MODEL), F32),
        compiler_params=_cparams(1),
        name="pe_gate",
    )(x2d, p2d, gain, wg_bf16, wp_bf16)


def kernel(x, p, positions, pre_mix_norm, w_in, gmlp_ln_g, gmlp_ln_b, gmlp_ws, gmlp_bs, conv_dw_w, conv_dw_b, conv_ln_g, conv_ln_b, conv_pw_w, conv_pw_b, diff_lq1, diff_lk1, diff_lq2, diff_lk2, diff_subln_g, out_norm_a, out_norm_b, out_norm_c, w_out, post_mix_norm, pre_ffn_norm, w_up, ffn_conv_w, ffn_conv_b, w_down, post_ffn_norm, pe_gate_norm, w_pe_gate, w_pe_proj):
    batch, seq, d = x.shape
    depth = w_in.shape[0]
    n = batch * seq
    row = lambda a: a.reshape(1, -1).astype(F32)

    x2d = x.reshape(n, d)
    tabs = _rope_tables(positions.reshape(n, 1))
    for i in range(depth):
        z = _inproj(x2d, row(pre_mix_norm[i]), w_in[i].astype(BF16), tabs)
        out_a = _moba(z, row(out_norm_a[i]), batch, seq)
        lvec = jnp.stack([diff_lq1[i], diff_lk1[i], diff_lq2[i], diff_lk2[i]]).astype(F32)
        lam_init = 0.8 - 0.6 * math.exp(-0.3 * i)
        out_d = _diff(z, lvec, row(jnp.tile(diff_subln_g[i], 2)), lam_init, batch, seq)
        bc_prm = (row(gmlp_ln_g[i]), row(gmlp_ln_b[i]),
                  gmlp_ws[i].reshape(N_HEADS * GMLP_CHUNK, GMLP_CHUNK),
                  jnp.repeat(gmlp_bs[i].T, HEAD_DIM, axis=1), row(out_norm_b[i]),
                  conv_dw_w[i], row(conv_dw_b[i]), row(conv_ln_g[i]), row(conv_ln_b[i]),
                  conv_pw_w[i].astype(BF16), row(conv_pw_b[i]), row(out_norm_c[i]))
        out_b, out_c = _gmlp_conv(z, bc_prm, batch, seq)
        x2d = _outproj(out_a, out_b, out_c, out_d, w_out[i].astype(BF16), x2d, row(post_mix_norm[i]))
        x2d = _ffn(x2d, row(pre_ffn_norm[i]), w_up[i].astype(BF16), ffn_conv_w[i], row(ffn_conv_b[i]),
                   w_down[i].astype(BF16), row(post_ffn_norm[i]), batch, seq)
        x2d = _pe(x2d, p[i].reshape(n, P_DIM), row(pe_gate_norm[i]),
                  w_pe_gate[i].astype(BF16), w_pe_proj[i].astype(BF16))
    return x2d.reshape(batch, seq, d)
```

```python
import functools
import math

import jax
import jax.numpy as jnp
from jax import lax
from jax.experimental import pallas as pl
from jax.experimental.pallas import tpu as pltpu

F32 = jnp.float32
BF16 = jnp.bfloat16

D_MODEL = 1024
P_DIM = 256
GW = 256
HEAD_DIM = 64
N_HEADS = 4
DIFF_QK_DIM = 32
MOBA_BLOCK = 256
MOBA_TOPK = 3
GMLP_CHUNK = 128
CONV_WIDTH = 31
D_FF = 2816
ROPE_THETA = 10000.0
EPS = 1e-6
LANES = 128
NEG_INF = float("-inf")
LOG2E = 1.4426950408889634

TM_PROJ = 512
TQ = 256
TK = 256
T_BC = 256
CONV_HALO = 32
FFN_HALO = 8
FF_CHUNK = 256
VMEM_LIMIT = 56 * 1024 * 1024

ZN_GROUPS = (1, 3, 4, 5, 6, 8)
ZT_GROUPS = (0, 2, 7, 9)
ZN_KA, ZN_BU, ZN_BV, ZN_CA, ZN_CG, ZN_KD = range(6)
ZT_QA, ZT_VA, ZT_QD, ZT_VD = range(4)


def _cparams(n_axes):
    return pltpu.CompilerParams(dimension_semantics=("arbitrary",) * n_axes,
                                vmem_limit_bytes=VMEM_LIMIT)


def _rms(xf, g, eps=EPS):
    return xf * lax.rsqrt(jnp.mean(xf * xf, axis=-1, keepdims=True) + eps) * g


def _ln(xf, g, b, eps=EPS):
    mu = jnp.mean(xf, axis=-1, keepdims=True)
    xc = xf - mu
    return xc * lax.rsqrt(jnp.mean(xc * xc, axis=-1, keepdims=True) + eps) * g + b


def _dot(a, b):
    return jnp.dot(a, b, preferred_element_type=F32)


def _rope_tables_kernel(pcol_ref, prow_ref, cosa_ref, sina_ref, cosd_ref, sind_ref,
                        cosat_ref, sinat_ref, cosdt_ref, sindt_ref):
    pcol = pcol_ref[...].astype(F32)
    prow = prow_ref[...].astype(F32)
    lane = lax.broadcasted_iota(jnp.int32, (1, LANES), 1)

    def inv_freq(idx, dim):
        return jnp.exp(idx.astype(F32) * (-2.0 / dim * math.log(ROPE_THETA)))

    def token_major(dim):
        half = dim // 2
        ang = pcol * inv_freq(lane % half, dim)
        sign = jnp.where((lane % dim) < half, -1.0, 1.0)
        return jnp.cos(ang), jnp.sin(ang) * sign

    def channel_major(dim):
        half = dim // 2
        ang = inv_freq(lax.broadcasted_iota(jnp.int32, (half, 1), 0), dim) * prow
        return jnp.cos(ang), jnp.sin(ang)

    cosa_ref[...], sina_ref[...] = token_major(HEAD_DIM)
    cosd_ref[...], sind_ref[...] = token_major(DIFF_QK_DIM)
    cosat_ref[...], sinat_ref[...] = channel_major(HEAD_DIM)
    cosdt_ref[...], sindt_ref[...] = channel_major(DIFF_QK_DIM)


def _rope_tables(positions):
    n = positions.size
    tm = 2048
    tok = jax.ShapeDtypeStruct((n, LANES), F32)
    tok_spec = pl.BlockSpec((tm, LANES), lambda i: (i, 0))
    ch = lambda half: jax.ShapeDtypeStruct((half, n), F32)
    ch_spec = lambda half: pl.BlockSpec((half, tm), lambda i: (0, i))
    ha, hd = HEAD_DIM // 2, DIFF_QK_DIM // 2
    return pl.pallas_call(
        _rope_tables_kernel,
        grid=(n // tm,),
        in_specs=[pl.BlockSpec((tm, 1), lambda i: (i, 0)), pl.BlockSpec((1, tm), lambda i: (0, i))],
        out_specs=[tok_spec] * 4 + [ch_spec(ha), ch_spec(ha), ch_spec(hd), ch_spec(hd)],
        out_shape=[tok] * 4 + [ch(ha), ch(ha), ch(hd), ch(hd)],
        compiler_params=_cparams(1),
        name="rope_tables",
    )(positions.reshape(n, 1), positions.reshape(1, n))


def _rope_slab(slab, cos, sin_signed, half):
    lane = lax.broadcasted_iota(jnp.int32, slab.shape, 1)
    first = (lane % (2 * half)) < half
    rot = jnp.where(first, pltpu.roll(slab, LANES - half, 1), pltpu.roll(slab, half, 1))
    return slab * cos + rot * sin_signed


def _inproj_kernel(x_ref, g_ref, wn_ref, wt_ref, cosa_ref, sina_ref, cosd_ref, sind_ref,
                   cosat_ref, sinat_ref, cosdt_ref, sindt_ref, zn_ref, zt_ref):
    tm = x_ref.shape[0]
    h = _rms(x_ref[...], g_ref[...]).astype(BF16)

    rope = {ZN_KA: (HEAD_DIM // 2, cosa_ref, sina_ref), ZN_KD: (DIFF_QK_DIM // 2, cosd_ref, sind_ref)}
    for j in range(len(ZN_GROUPS)):
        zj = _dot(h, wn_ref[:, j * GW:(j + 1) * GW])
        if j in rope:
            half, cos_ref, sin_ref = rope[j]
            for c in range(GW // LANES):
                slab = _rope_slab(zj[:, c * LANES:(c + 1) * LANES], cos_ref[...], sin_ref[...], half)
                zn_ref[:, j * GW + c * LANES:j * GW + (c + 1) * LANES] = slab.astype(BF16)
        else:
            zn_ref[:, j * GW:(j + 1) * GW] = zj.astype(BF16)

    zt = lax.dot_general(wt_ref[...], h, (((1,), (1,)), ((), ())), preferred_element_type=F32)

    def store_t(g, r0, val):
        for t in range(tm // TQ):
            zt_ref[g, t, r0:r0 + val.shape[0], :] = val[:, t * TQ:(t + 1) * TQ].astype(BF16)

    def rope_t(g, base, half, cos, sin, scale):
        r0 = g * GW + base
        x1, x2 = zt[r0:r0 + half], zt[r0 + half:r0 + 2 * half]
        store_t(g, base, (x1 * cos - x2 * sin) * scale)
        store_t(g, base + half, (x2 * cos + x1 * sin) * scale)

    cos_a, sin_a, cos_d, sin_d = cosat_ref[...], sinat_ref[...], cosdt_ref[...], sindt_ref[...]
    for hd in range(N_HEADS):
        rope_t(ZT_QA, hd * HEAD_DIM, HEAD_DIM // 2, cos_a, sin_a, HEAD_DIM ** -0.5)
    for c in range(2 * N_HEADS):
        rope_t(ZT_QD, c * DIFF_QK_DIM, DIFF_QK_DIM // 2, cos_d, sin_d, 1.0)
    store_t(ZT_VA, 0, zt[ZT_VA * GW:(ZT_VA + 1) * GW])
    store_t(ZT_VD, 0, zt[ZT_VD * GW:(ZT_VD + 1) * GW])


def _inproj(x2d, gain, wn_bf16, wt_bf16, tabs):
    n = x2d.shape[0]
    tm = TM_PROJ
    tok_spec = pl.BlockSpec((tm, LANES), lambda i: (i, 0))
    ch_spec = lambda half: pl.BlockSpec((half, tm), lambda i: (0, i))
    ha, hd = HEAD_DIM // 2, DIFF_QK_DIM // 2
    nzn, nzt = len(ZN_GROUPS), len(ZT_GROUPS)
    return pl.pallas_call(
        _inproj_kernel,
        grid=(n // tm,),
        in_specs=[pl.BlockSpec((tm, D_MODEL), lambda i: (i, 0)),
                  pl.BlockSpec((1, D_MODEL), lambda i: (0, 0)),
                  pl.BlockSpec((D_MODEL, nzn * GW), lambda i: (0, 0)),
                  pl.BlockSpec((nzt * GW, D_MODEL), lambda i: (0, 0)),
                  tok_spec, tok_spec, tok_spec, tok_spec,
                  ch_spec(ha), ch_spec(ha), ch_spec(hd), ch_spec(hd)],
        out_specs=[pl.BlockSpec((tm, nzn * GW), lambda i: (i, 0)),
                   pl.BlockSpec((nzt, tm // TQ, GW, TQ), lambda i: (0, i, 0, 0))],
        out_shape=[jax.ShapeDtypeStruct((n, nzn * GW), BF16),
                   jax.ShapeDtypeStruct((nzt, n // TQ, GW, TQ), BF16)],
        compiler_params=_cparams(1),
        name="in_proj",
    )(x2d, gain, wn_bf16, wt_bf16, *tabs)


def _pair_step(s, st, c0, wh, vt_heads, c, first):
    m_ref, l_ref, acc_ref = st
    cols = slice(c0, c0 + s.shape[1])
    m_new = jnp.max(s, axis=0, keepdims=True)
    if not first:
        m_old = m_ref[:, cols]
        m_new = jnp.maximum(m_old, m_new)
        alpha = jnp.exp2((m_old - m_new) * c)
    p = jnp.exp2((s - m_new) * c)
    p_sum = jnp.sum(p, axis=0, keepdims=True)
    p_bf16 = p.astype(BF16)
    m_ref[:, cols] = m_new
    l_ref[:, cols] = p_sum if first else alpha * l_ref[:, cols] + p_sum
    for i, vt_h in enumerate(vt_heads):
        pv = _dot(vt_h, p_bf16[:, i * wh:(i + 1) * wh])
        hc = slice(c0 + i * wh, c0 + (i + 1) * wh)
        acc_ref[:, hc] = pv if first else alpha[:, i * wh:(i + 1) * wh] * acc_ref[:, hc] + pv


def _causal_bias_t():
    key = lax.broadcasted_iota(jnp.int32, (TK, TQ), 0)
    qry = lax.broadcasted_iota(jnp.int32, (TK, TQ), 1)
    return jnp.where(key <= qry, 0.0, NEG_INF)


def _fill_qbd(qbd_ref, qt, width):
    row = lax.broadcasted_iota(jnp.int32, (LANES, TQ), 0)
    for p in range(GW // LANES):
        pair = qt[p * LANES:(p + 1) * LANES]
        for j in range(LANES // width):
            keep = (row >= j * width) & (row < (j + 1) * width)
            qbd_ref[p, :, j * TQ:(j + 1) * TQ] = jnp.where(keep, pair, jnp.zeros_like(pair))


def _moba_kernel(qt_ref, k_ref, vt_ref, gn_ref, o_ref, km_ref, fl_ref, qbd_ref, m_ref, l_ref, acc_ref):
    qi = pl.program_id(1)
    n_blocks = k_ref.shape[0] // MOBA_BLOCK
    st = (m_ref, l_ref, acc_ref)

    @pl.when(qi == 0)
    def _():
        lane = lax.broadcasted_iota(jnp.int32, (1, GW), 1)
        for n in range(n_blocks):
            kb = k_ref[n * MOBA_BLOCK:(n + 1) * MOBA_BLOCK, :].astype(F32)
            mean = jnp.sum(kb, axis=0, keepdims=True) * (1.0 / MOBA_BLOCK)
            for h in range(N_HEADS):
                km_ref[h * 8 + n:h * 8 + n + 1, :] = jnp.where(lane // HEAD_DIM == h, mean, 0.0)

    qt = qt_ref[...]
    g_t = jnp.dot(km_ref[...], qt.astype(F32), precision=lax.Precision.HIGHEST,
                  preferred_element_type=F32)
    blk = lax.broadcasted_iota(jnp.int32, (8, TQ), 0)
    n_past = jnp.full((8, TQ), qi, jnp.int32)
    for h in range(N_HEADS):
        g = g_t[h * 8:(h + 1) * 8, :]
        rank = jnp.zeros((8, TQ), F32)
        for m in range(n_blocks):
            gm = g[m:m + 1, :]
            beats = (gm > g) | ((gm == g) & (m < blk))
            rank = rank + jnp.where(beats & (m < n_past), 1.0, 0.0)
        fl_ref[:, h * TQ:(h + 1) * TQ] = jnp.where((rank < MOBA_TOPK) & (blk < n_past), 0.0, NEG_INF)

    _fill_qbd(qbd_ref, qt, HEAD_DIM)
    wp = (LANES // HEAD_DIM) * TQ

    def pair_tile(p, n, bias, first):
        start = pl.multiple_of(n * TK, TK)
        s = _dot(k_ref[pl.ds(start, TK), p * LANES:(p + 1) * LANES], qbd_ref[p]) + bias
        vt_heads = [vt_ref[n, h * HEAD_DIM:(h + 1) * HEAD_DIM, :] for h in (2 * p, 2 * p + 1)]
        _pair_step(s, st, p * wp, TQ, vt_heads, LOG2E, first)

    causal = jnp.concatenate([_causal_bias_t()] * (LANES // HEAD_DIM), axis=1)
    for p in range(GW // LANES):
        pair_tile(p, qi, causal, True)

    def past_block(n, carry):
        for p in range(GW // LANES):
            pair_tile(p, n, fl_ref[pl.ds(n, 1), p * wp:(p + 1) * wp], False)
        return carry

    lax.fori_loop(0, qi, past_block, 0)

    o_t = jnp.concatenate([acc_ref[:, h * TQ:(h + 1) * TQ] / l_ref[:, h * TQ:(h + 1) * TQ]
                           for h in range(N_HEADS)], axis=0)
    o_t = o_t * lax.rsqrt(jnp.mean(o_t * o_t, axis=0, keepdims=True) + EPS)
    o_ref[...] = (o_t.T * gn_ref[...]).astype(o_ref.dtype)


def _moba(zn, zt, gain, batch, seq):
    n = zn.shape[0]
    nq = seq // TQ
    return pl.pallas_call(
        _moba_kernel,
        grid=(batch, nq),
        in_specs=[pl.BlockSpec((None, None, GW, TQ), lambda b, i: (ZT_QA, b * nq + i, 0, 0)),
                  pl.BlockSpec((seq, GW), lambda b, i: (b, ZN_KA)),
                  pl.BlockSpec((None, seq // TK, GW, TK), lambda b, i: (ZT_VA, b, 0, 0)),
                  pl.BlockSpec((1, GW), lambda b, i: (0, 0))],
        out_specs=pl.BlockSpec((TQ, GW), lambda b, i: (b * nq + i, 0)),
        out_shape=jax.ShapeDtypeStruct((n, GW), BF16),
        scratch_shapes=[pltpu.VMEM((8 * N_HEADS, GW), F32),
                        pltpu.VMEM((8, N_HEADS * TQ), F32),
                        pltpu.VMEM((GW // LANES, LANES, N_HEADS // 2 * TQ), BF16),
                        pltpu.VMEM((1, N_HEADS * TQ), F32),
                        pltpu.VMEM((1, N_HEADS * TQ), F32),
                        pltpu.VMEM((HEAD_DIM, N_HEADS * TQ), F32)],
        compiler_params=_cparams(2),
        name="moba",
    )(zt, zn, zt, gain)


def _diff_kernel(lam_init, qt_ref, k_ref, vt_ref, lvec_ref, gs_ref, o_ref, qbd_ref, m_ref, l_ref, acc_ref):
    qi = pl.program_id(1)
    st = (m_ref, l_ref, acc_ref)
    c_exp = DIFF_QK_DIM ** -0.5 * LOG2E

    lv = lvec_ref[...]
    lam = (jnp.exp(jnp.sum(lv[0:1] * lv[1:2], axis=-1, keepdims=True))
           - jnp.exp(jnp.sum(lv[2:3] * lv[3:4], axis=-1, keepdims=True)) + lam_init)

    _fill_qbd(qbd_ref, qt_ref[...], DIFF_QK_DIM)
    wp = (LANES // DIFF_QK_DIM) * TQ

    def pair_tile(p, n, first):
        start = pl.multiple_of(n * TK, TK)
        s = _dot(k_ref[pl.ds(start, TK), p * LANES:(p + 1) * LANES], qbd_ref[p])
        if first:
            s = s + causal
        vt_heads = [vt_ref[n, h * HEAD_DIM:(h + 1) * HEAD_DIM, :] for h in (2 * p, 2 * p + 1)]
        _pair_step(s, st, p * wp, 2 * TQ, vt_heads, c_exp, first)

    causal = jnp.concatenate([_causal_bias_t()] * (LANES // DIFF_QK_DIM), axis=1)
    for p in range(GW // LANES):
        pair_tile(p, qi, True)

    def past_block(n, carry):
        for p in range(GW // LANES):
            pair_tile(p, n, False)
        return carry

    lax.fori_loop(0, qi, past_block, 0)

    heads = []
    for h in range(N_HEADS):
        c1, c2 = slice(2 * h * TQ, (2 * h + 1) * TQ), slice((2 * h + 1) * TQ, (2 * h + 2) * TQ)
        a1 = acc_ref[:, c1] / l_ref[:, c1]
        a2 = acc_ref[:, c2] / l_ref[:, c2]
        o = a1 - lam * a2
        heads.append(o * lax.rsqrt(jnp.mean(o * o, axis=0, keepdims=True) + 1e-5))
    o_t = jnp.concatenate(heads, axis=0)
    o_ref[...] = (o_t.T * gs_ref[...] * (1.0 - lam_init)).astype(o_ref.dtype)


def _diff(zn, zt, lvec, gs, lam_init, batch, seq):
    n = zn.shape[0]
    nq = seq // TQ
    return pl.pallas_call(
        functools.partial(_diff_kernel, lam_init),
        grid=(batch, nq),
        in_specs=[pl.BlockSpec((None, None, GW, TQ), lambda b, i: (ZT_QD, b * nq + i, 0, 0)),
                  pl.BlockSpec((seq, GW), lambda b, i: (b, ZN_KD)),
                  pl.BlockSpec((None, seq // TK, GW, TK), lambda b, i: (ZT_VD, b, 0, 0)),
                  pl.BlockSpec((4, DIFF_QK_DIM), lambda b, i: (0, 0)),
                  pl.BlockSpec((1, GW), lambda b, i: (0, 0))],
        out_specs=pl.BlockSpec((TQ, GW), lambda b, i: (b * nq + i, 0)),
        out_shape=jax.ShapeDtypeStruct((n, GW), BF16),
        scratch_shapes=[pltpu.VMEM((GW // LANES, LANES, N_HEADS * TQ), BF16),
                        pltpu.VMEM((1, 2 * N_HEADS * TQ), F32),
                        pltpu.VMEM((1, 2 * N_HEADS * TQ), F32),
                        pltpu.VMEM((HEAD_DIM, 2 * N_HEADS * TQ), F32)],
        compiler_params=_cparams(2),
        name="diff_attn",
    )(zt, zn, zt, lvec, gs)


def _gmlp_conv_kernel(zu_ref, zv_ref, za_ref, zg_ref,
                      lng_ref, lnb_ref, ws_ref, bsf_ref, gnb_ref,
                      dww_ref, dwb_ref, clg_ref, clb_ref, pww_ref, pwb_ref, gnc_ref,
                      ob_ref, oc_ref, ybuf_ref):
    ti = pl.program_id(1)
    t = zu_ref.shape[0]

    u = jax.nn.gelu(zu_ref[...].astype(F32))
    v = _ln(jax.nn.gelu(zv_ref[...].astype(F32)), lng_ref[...], lnb_ref[...])
    row = lax.broadcasted_iota(jnp.int32, ws_ref.shape, 0)
    col = lax.broadcasted_iota(jnp.int32, ws_ref.shape, 1)
    w = jnp.where(col <= (row % GMLP_CHUNK), ws_ref[...], 0.0).astype(BF16)
    lane = lax.broadcasted_iota(jnp.int32, (GMLP_CHUNK, GW), 1)
    for c in range(t // GMLP_CHUNK):
        rows = slice(c * GMLP_CHUNK, (c + 1) * GMLP_CHUNK)
        r = _dot(w, v[rows].astype(BF16))
        s = bsf_ref[...]
        for h in range(N_HEADS):
            s = s + jnp.where(lane // HEAD_DIM == h, r[h * GMLP_CHUNK:(h + 1) * GMLP_CHUNK], 0.0)
        ob_ref[rows, :] = _rms(u[rows] * s, gnb_ref[...]).astype(ob_ref.dtype)

    @pl.when(ti == 0)
    def _():
        ybuf_ref[0:CONV_HALO, :] = jnp.zeros((CONV_HALO, GW), F32)

    @pl.when(ti > 0)
    def _():
        ybuf_ref[0:CONV_HALO, :] = ybuf_ref[t:t + CONV_HALO, :]

    ybuf_ref[CONV_HALO:CONV_HALO + t, :] = za_ref[...].astype(F32) * jax.nn.sigmoid(zg_ref[...].astype(F32))
    acc = jnp.zeros((t, GW), F32) + dwb_ref[...]
    first = CONV_HALO - (CONV_WIDTH - 1)
    for k in range(CONV_WIDTH):
        acc = acc + dww_ref[k:k + 1, :] * ybuf_ref[first + k:first + k + t, :]
    y = jax.nn.silu(_ln(acc, clg_ref[...], clb_ref[...]))
    y = _dot(y.astype(BF16), pww_ref[...]) + pwb_ref[...]
    oc_ref[...] = _rms(y, gnc_ref[...]).astype(oc_ref.dtype)


def _gmlp_conv(zn, prm, batch, seq):
    n = zn.shape[0]
    nt = seq // T_BC
    zspec = lambda c: pl.BlockSpec((T_BC, GW), lambda b, i: (b * nt + i, c))
    const = lambda shape: pl.BlockSpec(shape, lambda b, i: (0, 0))
    out = jax.ShapeDtypeStruct((n, GW), BF16)
    return pl.pallas_call(
        _gmlp_conv_kernel,
        grid=(batch, nt),
        in_specs=[zspec(ZN_BU), zspec(ZN_BV), zspec(ZN_CA), zspec(ZN_CG),
                  const((1, GW)), const((1, GW)), const((N_HEADS * GMLP_CHUNK, GMLP_CHUNK)),
                  const((GMLP_CHUNK, GW)), const((1, GW)),
                  const((CONV_WIDTH, GW)), const((1, GW)), const((1, GW)), const((1, GW)),
                  const((GW, GW)), const((1, GW)), const((1, GW))],
        out_specs=[pl.BlockSpec((T_BC, GW), lambda b, i: (b * nt + i, 0))] * 2,
        out_shape=[out, out],
        scratch_shapes=[pltpu.VMEM((CONV_HALO + T_BC, GW), F32)],
        compiler_params=_cparams(2),
        name="gmlp_conv",
    )(zn, zn, zn, zn, *prm)


def _outproj_kernel(oa_ref, ob_ref, oc_ref, od_ref, w_ref, x_ref, g_ref, xo_ref):
    y = _dot(oa_ref[...], w_ref[0:GW, :])
    for j, o_ref in enumerate((ob_ref, oc_ref, od_ref), start=1):
        y = y + _dot(o_ref[...], w_ref[j * GW:(j + 1) * GW, :])
    xo_ref[...] = x_ref[...] + _rms(y, g_ref[...])


def _outproj(oa, ob, oc, od, w_bf16, x2d, gain):
    n = x2d.shape[0]
    tm = TM_PROJ
    ospec = pl.BlockSpec((tm, GW), lambda i: (i, 0))
    xspec = pl.BlockSpec((tm, D_MODEL), lambda i: (i, 0))
    return pl.pallas_call(
        _outproj_kernel,
        grid=(n // tm,),
        in_specs=[ospec, ospec, ospec, ospec,
                  pl.BlockSpec((4 * GW, D_MODEL), lambda i: (0, 0)),
                  xspec, pl.BlockSpec((1, D_MODEL), lambda i: (0, 0))],
        out_specs=xspec,
        out_shape=jax.ShapeDtypeStruct((n, D_MODEL), F32),
        compiler_params=_cparams(1),
        name="out_proj",
    )(oa, ob, oc, od, w_bf16, x2d, gain)


def _ffn_kernel(x_ref, gpre_ref, wup_ref, cw_ref, cb_ref, wdn_ref, gpost_ref, xo_ref,
                carry_ref, ubuf_ref, acc_ref):
    ti = pl.program_id(1)
    tm = x_ref.shape[0]

    @pl.when(ti == 0)
    def _():
        carry_ref[...] = jnp.zeros_like(carry_ref)

    x = x_ref[...]
    h = _rms(x, gpre_ref[...]).astype(BF16)

    def conv_cols(c0):
        cols = slice(c0, c0 + FF_CHUNK)
        u = _dot(h, wup_ref[:, cols])
        ubuf_ref[0:FFN_HALO, :] = carry_ref[:, cols]
        ubuf_ref[FFN_HALO:FFN_HALO + tm, :] = u
        carry_ref[:, cols] = u[tm - FFN_HALO:tm, :]
        return (cw_ref[0:1, cols] * ubuf_ref[FFN_HALO - 2:FFN_HALO - 2 + tm, :]
                + cw_ref[1:2, cols] * ubuf_ref[FFN_HALO - 1:FFN_HALO - 1 + tm, :]
                + cw_ref[2:3, cols] * u + cb_ref[:, cols])

    for j in range(D_FF // FF_CHUNK):
        gate = conv_cols(j * FF_CHUNK)
        val = conv_cols(D_FF + j * FF_CHUNK)
        act = (jax.nn.gelu(gate) * val).astype(BF16)
        part = _dot(act, wdn_ref[j * FF_CHUNK:(j + 1) * FF_CHUNK, :])
        if j == 0:
            acc_ref[...] = part
        else:
            acc_ref[...] += part
    xo_ref[...] = x + _rms(acc_ref[...], gpost_ref[...])


def _ffn(x2d, gpre, wup_bf16, cw, cb, wdn_bf16, gpost, batch, seq):
    n = x2d.shape[0]
    tm = TM_PROJ
    nt = seq // tm
    xspec = pl.BlockSpec((tm, D_MODEL), lambda b, i: (b * nt + i, 0))
    const = lambda shape: pl.BlockSpec(shape, lambda b, i: (0, 0), pipeline_mode=pl.Buffered(1))
    return pl.pallas_call(
        _ffn_kernel,
        grid=(batch, nt),
        in_specs=[xspec, const((1, D_MODEL)), const((D_MODEL, 2 * D_FF)),
                  const((3, 2 * D_FF)), const((1, 2 * D_FF)),
                  const((D_FF, D_MODEL)), const((1, D_MODEL))],
        out_specs=xspec,
        out_shape=jax.ShapeDtypeStruct((n, D_MODEL), F32),
        scratch_shapes=[pltpu.VMEM((FFN_HALO, 2 * D_FF), F32),
                        pltpu.VMEM((FFN_HALO + tm, FF_CHUNK), F32),
                        pltpu.VMEM((tm, D_MODEL), F32)],
        compiler_params=_cparams(2),
        name="ffn",
    )(x2d, gpre, wup_bf16, cw, cb, wdn_bf16, gpost)


def _pe_kernel(x_ref, p_ref, g_ref, wg_ref, wp_ref, xo_ref):
    x = x_ref[...]
    gate = jax.nn.sigmoid(_dot(_rms(x, g_ref[...]).astype(BF16), wg_ref[...]))
    proj = _dot(p_ref[...].astype(BF16), wp_ref[...])
    xo_ref[...] = x + gate * proj


def _pe(x2d, p2d, gain, wg_bf16, wp_bf16):
    n = x2d.shape[0]
    tm = TM_PROJ
    xspec = pl.BlockSpec((tm, D_MODEL), lambda i: (i, 0))
    return pl.pallas_call(
        _pe_kernel,
        grid=(n // tm,),
        in_specs=[xspec, pl.BlockSpec((tm, P_DIM), lambda i: (i, 0)),
                  pl.BlockSpec((1, D_MODEL), lambda i: (0, 0)),
                  pl.BlockSpec((D_MODEL, D_MODEL), lambda i: (0, 0)),
                  pl.BlockSpec((P_DIM, D_MODEL), lambda i: (0, 0))],
        out_specs=xspec,
        out_shape=jax.ShapeDtypeStruct((n, D_MODEL), F32),
        compiler_params=_cparams(1),
        name="pe_gate",
    )(x2d, p2d, gain, wg_bf16, wp_bf16)


def kernel(x, p, positions, pre_mix_norm, w_in, gmlp_ln_g, gmlp_ln_b, gmlp_ws, gmlp_bs, conv_dw_w, conv_dw_b, conv_ln_g, conv_ln_b, conv_pw_w, conv_pw_b, diff_lq1, diff_lk1, diff_lq2, diff_lk2, diff_subln_g, out_norm_a, out_norm_b, out_norm_c, w_out, post_mix_norm, pre_ffn_norm, w_up, ffn_conv_w, ffn_conv_b, w_down, post_ffn_norm, pe_gate_norm, w_pe_gate, w_pe_proj):
    batch, seq, d = x.shape
    depth = w_in.shape[0]
    n = batch * seq
    row = lambda a: a.reshape(1, -1).astype(F32)

    x2d = x.reshape(n, d)
    tabs = _rope_tables(positions)
    for i in range(depth):
        w_bf16 = w_in[i].astype(BF16)
        group = lambda g: w_bf16[:, g * GW:(g + 1) * GW]
        wn = jnp.concatenate([group(g) for g in ZN_GROUPS], axis=1)
        wt = jnp.concatenate([group(g) for g in ZT_GROUPS], axis=1).T
        zn, zt = _inproj(x2d, row(pre_mix_norm[i]), wn, wt, tabs)
        out_a = _moba(zn, zt, row(out_norm_a[i]), batch, seq)
        lvec = jnp.stack([diff_lq1[i], diff_lk1[i], diff_lq2[i], diff_lk2[i]]).astype(F32)
        lam_init = 0.8 - 0.6 * math.exp(-0.3 * i)
        out_d = _diff(zn, zt, lvec, row(jnp.tile(diff_subln_g[i], N_HEADS)), lam_init, batch, seq)
        bc_prm = (row(gmlp_ln_g[i]), row(gmlp_ln_b[i]),
                  gmlp_ws[i].reshape(N_HEADS * GMLP_CHUNK, GMLP_CHUNK),
                  jnp.repeat(gmlp_bs[i].T, HEAD_DIM, axis=1), row(out_norm_b[i]),
                  conv_dw_w[i], row(conv_dw_b[i]), row(conv_ln_g[i]), row(conv_ln_b[i]),
                  conv_pw_w[i].astype(BF16), row(conv_pw_b[i]), row(out_norm_c[i]))
        out_b, out_c = _gmlp_conv(zn, bc_prm, batch, seq)
        x2d = _outproj(out_a, out_b, out_c, out_d, w_out[i].astype(BF16), x2d, row(post_mix_norm[i]))
        x2d = _ffn(x2d, row(pre_ffn_norm[i]), w_up[i].astype(BF16), ffn_conv_w[i], row(ffn_conv_b[i]),
                   w_down[i].astype(BF16), row(post_ffn_norm[i]), batch, seq)
        x2d = _pe(x2d, p[i].reshape(n, P_DIM), row(pe_gate_norm[i]),
                  w_pe_gate[i].astype(BF16), w_pe_proj[i].astype(BF16))
    return x2d.reshape(batch, seq, d)
```

```python
import functools
import math

import jax
import jax.numpy as jnp
from jax import lax
from jax.experimental import pallas as pl
from jax.experimental.pallas import tpu as pltpu

F32 = jnp.float32
BF16 = jnp.bfloat16

D_MODEL = 1024
P_DIM = 256
GW = 256
HEAD_DIM = 64
N_HEADS = 4
DIFF_QK_DIM = 32
MOBA_BLOCK = 256
MOBA_TOPK = 3
GMLP_CHUNK = 128
CONV_WIDTH = 31
D_FF = 2816
ROPE_THETA = 10000.0
EPS = 1e-6
LANES = 128
NEG_INF = float("-inf")
LOG2E = 1.4426950408889634

TM_PROJ = 512
TQ = 256
TK = 256
T_BC = 256
CONV_HALO = 32
FFN_HALO = 8
FF_CHUNK = 256
FFN_UBUFS = 4
VMEM_LIMIT = 56 * 1024 * 1024

ZN_GROUPS = (1, 3, 4, 5, 6, 8)
ZT_GROUPS = (0, 2, 7, 9)
ZN_KA, ZN_BU, ZN_BV, ZN_CA, ZN_CG, ZN_KD = range(6)
ZT_QA, ZT_VA, ZT_QD, ZT_VD = range(4)


def _cparams(n_axes):
    return pltpu.CompilerParams(dimension_semantics=("arbitrary",) * n_axes,
                                vmem_limit_bytes=VMEM_LIMIT)


def _rms(xf, g, eps=EPS):
    return xf * lax.rsqrt(jnp.mean(xf * xf, axis=-1, keepdims=True) + eps) * g


def _ln(xf, g, b, eps=EPS):
    mu = jnp.mean(xf, axis=-1, keepdims=True)
    xc = xf - mu
    return xc * lax.rsqrt(jnp.mean(xc * xc, axis=-1, keepdims=True) + eps) * g + b


def _dot(a, b):
    return jnp.dot(a, b, preferred_element_type=F32)


def _rope_tables_kernel(prow_ref, cosa_ref, sina_ref, cosd_ref, sind_ref,
                        cosat_ref, sinat_ref, cosdt_ref, sindt_ref):
    prow = prow_ref[...].astype(F32)

    def channel_major(dim):
        half = dim // 2
        idx = lax.broadcasted_iota(jnp.int32, (half, 1), 0).astype(F32)
        inv = jnp.exp(idx * (-2.0 / dim * math.log(ROPE_THETA)))
        ang = inv * prow
        return jnp.cos(ang), jnp.sin(ang)

    def token_major(cos_t, sin_t):
        reps = LANES // (2 * cos_t.shape[0])
        return (jnp.concatenate([cos_t, cos_t] * reps, axis=0).T,
                jnp.concatenate([-sin_t, sin_t] * reps, axis=0).T)

    cos_a, sin_a = channel_major(HEAD_DIM)
    cos_d, sin_d = channel_major(DIFF_QK_DIM)
    cosat_ref[...], sinat_ref[...] = cos_a, sin_a
    cosdt_ref[...], sindt_ref[...] = cos_d, sin_d
    cosa_ref[...], sina_ref[...] = token_major(cos_a, sin_a)
    cosd_ref[...], sind_ref[...] = token_major(cos_d, sin_d)


def _rope_tables(positions):
    n = positions.size
    tm = 2048
    tok = jax.ShapeDtypeStruct((n, LANES), F32)
    tok_spec = pl.BlockSpec((tm, LANES), lambda i: (i, 0))
    ch = lambda half: jax.ShapeDtypeStruct((half, n), F32)
    ch_spec = lambda half: pl.BlockSpec((half, tm), lambda i: (0, i))
    ha, hd = HEAD_DIM // 2, DIFF_QK_DIM // 2
    return pl.pallas_call(
        _rope_tables_kernel,
        grid=(n // tm,),
        in_specs=[pl.BlockSpec((1, tm), lambda i: (0, i))],
        out_specs=[tok_spec] * 4 + [ch_spec(ha), ch_spec(ha), ch_spec(hd), ch_spec(hd)],
        out_shape=[tok] * 4 + [ch(ha), ch(ha), ch(hd), ch(hd)],
        compiler_params=_cparams(1),
        name="rope_tables",
    )(positions.reshape(1, n))


def _rope_slab(slab, cos, sin_signed, half):
    lane = lax.broadcasted_iota(jnp.int32, slab.shape, 1)
    first = (lane % (2 * half)) < half
    rot = jnp.where(first, pltpu.roll(slab, LANES - half, 1), pltpu.roll(slab, half, 1))
    return slab * cos + rot * sin_signed


def _inproj_kernel(x_ref, g_ref, wn_ref, wt_ref, cosa_ref, sina_ref, cosd_ref, sind_ref,
                   cosat_ref, sinat_ref, cosdt_ref, sindt_ref, zn_ref, zt_ref):
    tm = x_ref.shape[0]
    h = _rms(x_ref[...], g_ref[...]).astype(BF16)

    rope = {ZN_KA: (HEAD_DIM // 2, cosa_ref, sina_ref), ZN_KD: (DIFF_QK_DIM // 2, cosd_ref, sind_ref)}
    for j in range(len(ZN_GROUPS)):
        zj = _dot(h, wn_ref[:, j * GW:(j + 1) * GW])
        if j in rope:
            half, cos_ref, sin_ref = rope[j]
            for c in range(GW // LANES):
                slab = _rope_slab(zj[:, c * LANES:(c + 1) * LANES], cos_ref[...], sin_ref[...], half)
                zn_ref[:, j * GW + c * LANES:j * GW + (c + 1) * LANES] = slab.astype(BF16)
        else:
            zn_ref[:, j * GW:(j + 1) * GW] = zj.astype(BF16)

    zt = lax.dot_general(wt_ref[...], h, (((1,), (1,)), ((), ())), preferred_element_type=F32)

    def store_t(g, r0, val):
        for t in range(tm // TQ):
            zt_ref[g, t, r0:r0 + val.shape[0], :] = val[:, t * TQ:(t + 1) * TQ].astype(BF16)

    def rope_t(g, base, half, cos, sin, scale):
        r0 = g * GW + base
        x1, x2 = zt[r0:r0 + half], zt[r0 + half:r0 + 2 * half]
        store_t(g, base, (x1 * cos - x2 * sin) * scale)
        store_t(g, base + half, (x2 * cos + x1 * sin) * scale)

    cos_a, sin_a, cos_d, sin_d = cosat_ref[...], sinat_ref[...], cosdt_ref[...], sindt_ref[...]
    for hd in range(N_HEADS):
        rope_t(ZT_QA, hd * HEAD_DIM, HEAD_DIM // 2, cos_a, sin_a, HEAD_DIM ** -0.5)
    for c in range(2 * N_HEADS):
        rope_t(ZT_QD, c * DIFF_QK_DIM, DIFF_QK_DIM // 2, cos_d, sin_d, 1.0)
    store_t(ZT_VA, 0, zt[ZT_VA * GW:(ZT_VA + 1) * GW])
    store_t(ZT_VD, 0, zt[ZT_VD * GW:(ZT_VD + 1) * GW])


def _inproj(x2d, gain, wn_bf16, wt_bf16, tabs):
    n = x2d.shape[0]
    tm = TM_PROJ
    tok_spec = pl.BlockSpec((tm, LANES), lambda i: (i, 0))
    ch_spec = lambda half: pl.BlockSpec((half, tm), lambda i: (0, i))
    ha, hd = HEAD_DIM // 2, DIFF_QK_DIM // 2
    nzn, nzt = len(ZN_GROUPS), len(ZT_GROUPS)
    return pl.pallas_call(
        _inproj_kernel,
        grid=(n // tm,),
        in_specs=[pl.BlockSpec((tm, D_MODEL), lambda i: (i, 0)),
                  pl.BlockSpec((1, D_MODEL), lambda i: (0, 0)),
                  pl.BlockSpec((D_MODEL, nzn * GW), lambda i: (0, 0)),
                  pl.BlockSpec((nzt * GW, D_MODEL), lambda i: (0, 0)),
                  tok_spec, tok_spec, tok_spec, tok_spec,
                  ch_spec(ha), ch_spec(ha), ch_spec(hd), ch_spec(hd)],
        out_specs=[pl.BlockSpec((tm, nzn * GW), lambda i: (i, 0)),
                   pl.BlockSpec((nzt, tm // TQ, GW, TQ), lambda i: (0, i, 0, 0))],
        out_shape=[jax.ShapeDtypeStruct((n, nzn * GW), BF16),
                   jax.ShapeDtypeStruct((nzt, n // TQ, GW, TQ), BF16)],
        compiler_params=_cparams(1),
        name="in_proj",
    )(x2d, gain, wn_bf16, wt_bf16, *tabs)


def _init_state(st):
    m_ref, l_ref, acc_ref = st
    m_ref[...] = jnp.full(m_ref.shape, NEG_INF, F32)
    l_ref[...] = jnp.zeros(l_ref.shape, F32)
    acc_ref[...] = jnp.zeros(acc_ref.shape, F32)


def _pair_step(s, st, c0, wh, vt_heads, c):
    m_ref, l_ref, acc_ref = st
    cols = slice(c0, c0 + s.shape[1])
    m_old = m_ref[:, cols]
    m_new = jnp.maximum(m_old, jnp.max(s, axis=0, keepdims=True))
    m_safe = jnp.where(m_new == NEG_INF, 0.0, m_new)
    alpha = jnp.exp2((m_old - m_safe) * c)
    p = jnp.exp2((s - m_safe) * c)
    p_bf16 = p.astype(BF16)
    m_ref[:, cols] = m_new
    l_ref[:, cols] = alpha * l_ref[:, cols] + jnp.sum(p, axis=0, keepdims=True)
    for i, vt_h in enumerate(vt_heads):
        hc = slice(c0 + i * wh, c0 + (i + 1) * wh)
        acc_ref[:, hc] = (alpha[:, i * wh:(i + 1) * wh] * acc_ref[:, hc]
                          + _dot(vt_h, p_bf16[:, i * wh:(i + 1) * wh]))


def _attend(qi, qk, consume_past, consume_diag):
    qk(0, 0)

    def two_tiles(j, carry):
        n = 2 * j
        qk(n + 1, 1)
        consume_past(n, 0)
        qk(n + 2, 0)
        consume_past(n + 1, 1)
        return carry

    lax.fori_loop(0, lax.shift_right_logical(qi, 1), two_tiles, 0)

    @pl.when(jnp.bitwise_and(qi, 1) == 0)
    def _():
        consume_diag(0)

    @pl.when(jnp.bitwise_and(qi, 1) == 1)
    def _():
        qk(qi, 1)
        consume_past(qi - 1, 0)
        consume_diag(1)


def _causal_bias_t():
    key = lax.broadcasted_iota(jnp.int32, (TK, TQ), 0)
    qry = lax.broadcasted_iota(jnp.int32, (TK, TQ), 1)
    return jnp.where(key <= qry, 0.0, NEG_INF)


def _fill_qbd(qbd_ref, qt, width):
    row = lax.broadcasted_iota(jnp.int32, (LANES, TQ), 0)
    for p in range(GW // LANES):
        pair = qt[p * LANES:(p + 1) * LANES]
        for j in range(LANES // width):
            keep = (row >= j * width) & (row < (j + 1) * width)
            qbd_ref[p, :, j * TQ:(j + 1) * TQ] = jnp.where(keep, pair, jnp.zeros_like(pair))


def _moba_kernel(qt_ref, k_ref, vt_ref, gn_ref, o_ref,
                 km_ref, fl_ref, qbd_ref, sa_ref, sb_ref, m_ref, l_ref, acc_ref):
    qi = pl.program_id(1)
    n_blocks = k_ref.shape[0] // MOBA_BLOCK
    st = (m_ref, l_ref, acc_ref)

    @pl.when(qi == 0)
    def _():
        lane = lax.broadcasted_iota(jnp.int32, (1, GW), 1)
        for n in range(n_blocks):
            kb = k_ref[n * MOBA_BLOCK:(n + 1) * MOBA_BLOCK, :].astype(F32)
            mean = jnp.sum(kb, axis=0, keepdims=True) * (1.0 / MOBA_BLOCK)
            for h in range(N_HEADS):
                km_ref[h * 8 + n:h * 8 + n + 1, :] = jnp.where(lane // HEAD_DIM == h, mean, 0.0)

    qt = qt_ref[...]
    g_t = jnp.dot(km_ref[...], qt.astype(F32), precision=lax.Precision.HIGHEST,
                  preferred_element_type=F32)
    blk = lax.broadcasted_iota(jnp.int32, (8, TQ), 0)
    n_past = jnp.full((8, TQ), qi, jnp.int32)
    for h in range(N_HEADS):
        g = g_t[h * 8:(h + 1) * 8, :]
        rank = jnp.zeros((8, TQ), F32)
        for m in range(n_blocks):
            gm = g[m:m + 1, :]
            beats = (gm > g) | ((gm == g) & (m < blk))
            rank = rank + jnp.where(beats & (m < n_past), 1.0, 0.0)
        fl_ref[:, h * TQ:(h + 1) * TQ] = jnp.where((rank < MOBA_TOPK) & (blk < n_past), 0.0, NEG_INF)

    _fill_qbd(qbd_ref, qt, HEAD_DIM)
    wp = (LANES // HEAD_DIM) * TQ

    s_bufs = (sa_ref, sb_ref)
    pairs = range(GW // LANES)
    _init_state(st)

    def qk(n, buf):
        start = pl.multiple_of(n * TK, TK)
        for p in pairs:
            s_bufs[buf][p] = _dot(k_ref[pl.ds(start, TK), p * LANES:(p + 1) * LANES], qbd_ref[p])

    def consume(n, buf, bias):
        for p in pairs:
            vt_heads = [vt_ref[n, h * HEAD_DIM:(h + 1) * HEAD_DIM, :] for h in (2 * p, 2 * p + 1)]
            _pair_step(s_bufs[buf][p] + bias(p), st, p * wp, TQ, vt_heads, LOG2E)

    def consume_past(n, buf):
        consume(n, buf, lambda p: fl_ref[pl.ds(n, 1), p * wp:(p + 1) * wp])

    def consume_diag(buf):
        consume(qi, buf, lambda p: jnp.concatenate([_causal_bias_t()] * (LANES // HEAD_DIM), axis=1))

    _attend(qi, qk, consume_past, consume_diag)

    o_t = jnp.concatenate([acc_ref[:, h * TQ:(h + 1) * TQ] / l_ref[:, h * TQ:(h + 1) * TQ]
                           for h in range(N_HEADS)], axis=0)
    o_t = o_t * lax.rsqrt(jnp.mean(o_t * o_t, axis=0, keepdims=True) + EPS)
    o_ref[...] = (o_t.T * gn_ref[...]).astype(o_ref.dtype)


def _moba(zn, zt, gain, batch, seq):
    n = zn.shape[0]
    nq = seq // TQ
    return pl.pallas_call(
        _moba_kernel,
        grid=(batch, nq),
        in_specs=[pl.BlockSpec((None, None, GW, TQ), lambda b, i: (ZT_QA, b * nq + i, 0, 0)),
                  pl.BlockSpec((seq, GW), lambda b, i: (b, ZN_KA)),
                  pl.BlockSpec((None, seq // TK, GW, TK), lambda b, i: (ZT_VA, b, 0, 0)),
                  pl.BlockSpec((1, GW), lambda b, i: (0, 0))],
        out_specs=pl.BlockSpec((TQ, GW), lambda b, i: (b * nq + i, 0)),
        out_shape=jax.ShapeDtypeStruct((n, GW), BF16),
        scratch_shapes=[pltpu.VMEM((8 * N_HEADS, GW), F32),
                        pltpu.VMEM((8, N_HEADS * TQ), F32),
                        pltpu.VMEM((GW // LANES, LANES, N_HEADS // 2 * TQ), BF16),
                        pltpu.VMEM((GW // LANES, TK, N_HEADS // 2 * TQ), F32),
                        pltpu.VMEM((GW // LANES, TK, N_HEADS // 2 * TQ), F32),
                        pltpu.VMEM((1, N_HEADS * TQ), F32),
                        pltpu.VMEM((1, N_HEADS * TQ), F32),
                        pltpu.VMEM((HEAD_DIM, N_HEADS * TQ), F32)],
        compiler_params=_cparams(2),
        name="moba",
    )(zt, zn, zt, gain)


def _diff_kernel(lam_init, qt_ref, k_ref, vt_ref, lvec_ref, gs_ref, o_ref,
                 qbd_ref, sa_ref, sb_ref, m_ref, l_ref, acc_ref):
    qi = pl.program_id(1)
    st = (m_ref, l_ref, acc_ref)
    c_exp = DIFF_QK_DIM ** -0.5 * LOG2E

    lv = lvec_ref[...]
    lam = (jnp.exp(jnp.sum(lv[0:1] * lv[1:2], axis=-1, keepdims=True))
           - jnp.exp(jnp.sum(lv[2:3] * lv[3:4], axis=-1, keepdims=True)) + lam_init)

    _fill_qbd(qbd_ref, qt_ref[...], DIFF_QK_DIM)
    wp = (LANES // DIFF_QK_DIM) * TQ

    s_bufs = (sa_ref, sb_ref)
    pairs = range(GW // LANES)
    _init_state(st)

    def qk(n, buf):
        start = pl.multiple_of(n * TK, TK)
        for p in pairs:
            s_bufs[buf][p] = _dot(k_ref[pl.ds(start, TK), p * LANES:(p + 1) * LANES], qbd_ref[p])

    def consume(n, buf, diag):
        for p in pairs:
            s = s_bufs[buf][p]
            if diag:
                s = s + jnp.concatenate([_causal_bias_t()] * (LANES // DIFF_QK_DIM), axis=1)
            vt_heads = [vt_ref[n, h * HEAD_DIM:(h + 1) * HEAD_DIM, :] for h in (2 * p, 2 * p + 1)]
            _pair_step(s, st, p * wp, 2 * TQ, vt_heads, c_exp)

    _attend(qi, qk, lambda n, buf: consume(n, buf, False), lambda buf: consume(qi, buf, True))

    heads = []
    for h in range(N_HEADS):
        c1, c2 = slice(2 * h * TQ, (2 * h + 1) * TQ), slice((2 * h + 1) * TQ, (2 * h + 2) * TQ)
        a1 = acc_ref[:, c1] / l_ref[:, c1]
        a2 = acc_ref[:, c2] / l_ref[:, c2]
        o = a1 - lam * a2
        heads.append(o * lax.rsqrt(jnp.mean(o * o, axis=0, keepdims=True) + 1e-5))
    o_t = jnp.concatenate(heads, axis=0)
    o_ref[...] = (o_t.T * gs_ref[...] * (1.0 - lam_init)).astype(o_ref.dtype)


def _diff(zn, zt, lvec, gs, lam_init, batch, seq):
    n = zn.shape[0]
    nq = seq // TQ
    return pl.pallas_call(
        functools.partial(_diff_kernel, lam_init),
        grid=(batch, nq),
        in_specs=[pl.BlockSpec((None, None, GW, TQ), lambda b, i: (ZT_QD, b * nq + i, 0, 0)),
                  pl.BlockSpec((seq, GW), lambda b, i: (b, ZN_KD)),
                  pl.BlockSpec((None, seq // TK, GW, TK), lambda b, i: (ZT_VD, b, 0, 0)),
                  pl.BlockSpec((4, DIFF_QK_DIM), lambda b, i: (0, 0)),
                  pl.BlockSpec((1, GW), lambda b, i: (0, 0))],
        out_specs=pl.BlockSpec((TQ, GW), lambda b, i: (b * nq + i, 0)),
        out_shape=jax.ShapeDtypeStruct((n, GW), BF16),
        scratch_shapes=[pltpu.VMEM((GW // LANES, LANES, N_HEADS * TQ), BF16),
                        pltpu.VMEM((GW // LANES, TK, N_HEADS * TQ), F32),
                        pltpu.VMEM((GW // LANES, TK, N_HEADS * TQ), F32),
                        pltpu.VMEM((1, 2 * N_HEADS * TQ), F32),
                        pltpu.VMEM((1, 2 * N_HEADS * TQ), F32),
                        pltpu.VMEM((HEAD_DIM, 2 * N_HEADS * TQ), F32)],
        compiler_params=_cparams(2),
        name="diff_attn",
    )(zt, zn, zt, lvec, gs)


def _gmlp_conv_kernel(zu_ref, zv_ref, za_ref, zg_ref,
                      lng_ref, lnb_ref, ws_ref, bsf_ref, gnb_ref,
                      dww_ref, dwb_ref, clg_ref, clb_ref, pww_ref, pwb_ref, gnc_ref,
                      ob_ref, oc_ref, ybuf_ref):
    ti = pl.program_id(1)
    t = zu_ref.shape[0]

    u = jax.nn.gelu(zu_ref[...].astype(F32))
    v = _ln(jax.nn.gelu(zv_ref[...].astype(F32)), lng_ref[...], lnb_ref[...])
    row = lax.broadcasted_iota(jnp.int32, ws_ref.shape, 0)
    col = lax.broadcasted_iota(jnp.int32, ws_ref.shape, 1)
    w = jnp.where(col <= (row % GMLP_CHUNK), ws_ref[...], 0.0).astype(BF16)
    lane = lax.broadcasted_iota(jnp.int32, (GMLP_CHUNK, GW), 1)
    for c in range(t // GMLP_CHUNK):
        rows = slice(c * GMLP_CHUNK, (c + 1) * GMLP_CHUNK)
        r = _dot(w, v[rows].astype(BF16))
        s = bsf_ref[...]
        for h in range(N_HEADS):
            s = s + jnp.where(lane // HEAD_DIM == h, r[h * GMLP_CHUNK:(h + 1) * GMLP_CHUNK], 0.0)
        ob_ref[rows, :] = _rms(u[rows] * s, gnb_ref[...]).astype(ob_ref.dtype)

    @pl.when(ti == 0)
    def _():
        ybuf_ref[0:CONV_HALO, :] = jnp.zeros((CONV_HALO, GW), F32)

    @pl.when(ti > 0)
    def _():
        ybuf_ref[0:CONV_HALO, :] = ybuf_ref[t:t + CONV_HALO, :]

    ybuf_ref[CONV_HALO:CONV_HALO + t, :] = za_ref[...].astype(F32) * jax.nn.sigmoid(zg_ref[...].astype(F32))
    acc = jnp.zeros((t, GW), F32) + dwb_ref[...]
    first = CONV_HALO - (CONV_WIDTH - 1)
    for r in range(8):
        offs = [o for o in range(first, first + CONV_WIDTH) if o % 8 == r]
        slab = ybuf_ref[r:r + (offs[-1] - r) + t, :]
        for o in offs:
            k = o - first
            acc = acc + dww_ref[k:k + 1, :] * slab[o - r:o - r + t]
    y = jax.nn.silu(_ln(acc, clg_ref[...], clb_ref[...]))
    y = _dot(y.astype(BF16), pww_ref[...]) + pwb_ref[...]
    oc_ref[...] = _rms(y, gnc_ref[...]).astype(oc_ref.dtype)


def _gmlp_conv(zn, prm, batch, seq):
    n = zn.shape[0]
    nt = seq // T_BC
    zspec = lambda c: pl.BlockSpec((T_BC, GW), lambda b, i: (b * nt + i, c))
    const = lambda shape: pl.BlockSpec(shape, lambda b, i: (0, 0))
    out = jax.ShapeDtypeStruct((n, GW), BF16)
    return pl.pallas_call(
        _gmlp_conv_kernel,
        grid=(batch, nt),
        in_specs=[zspec(ZN_BU), zspec(ZN_BV), zspec(ZN_CA), zspec(ZN_CG),
                  const((1, GW)), const((1, GW)), const((N_HEADS * GMLP_CHUNK, GMLP_CHUNK)),
                  const((GMLP_CHUNK, GW)), const((1, GW)),
                  const((CONV_WIDTH, GW)), const((1, GW)), const((1, GW)), const((1, GW)),
                  const((GW, GW)), const((1, GW)), const((1, GW))],
        out_specs=[pl.BlockSpec((T_BC, GW), lambda b, i: (b * nt + i, 0))] * 2,
        out_shape=[out, out],
        scratch_shapes=[pltpu.VMEM((CONV_HALO + T_BC, GW), F32)],
        compiler_params=_cparams(2),
        name="gmlp_conv",
    )(zn, zn, zn, zn, *prm)


def _post_kernel(oa_ref, ob_ref, oc_ref, od_ref, x_ref, p_ref,
                 wout_ref, gmix_ref, gpre_ref, wup_ref, cw_ref, cb_ref, wdn_ref, gpost_ref,
                 gpe_ref, wg_ref, wp_ref, xo_ref, carry_ref, ubuf_ref, acc_ref):
    ti = pl.program_id(1)
    tm = x_ref.shape[0]
    n_chunks = D_FF // FF_CHUNK

    @pl.when(ti == 0)
    def _():
        carry_ref[...] = jnp.zeros_like(carry_ref)

    y = _dot(oa_ref[...], wout_ref[0:GW, :])
    for j, o_ref in enumerate((ob_ref, oc_ref, od_ref), start=1):
        y = y + _dot(o_ref[...], wout_ref[j * GW:(j + 1) * GW, :])
    x1 = x_ref[...] + _rms(y, gmix_ref[...])
    h = _rms(x1, gpre_ref[...]).astype(BF16)

    def up_cols(c0, slot):
        cols = slice(c0, c0 + FF_CHUNK)
        ubuf = ubuf_ref.at[slot]
        u = _dot(h, wup_ref[:, cols])
        ubuf[0:FFN_HALO, :] = carry_ref[:, cols]
        ubuf[FFN_HALO:FFN_HALO + tm, :] = u
        carry_ref[:, cols] = u[tm - FFN_HALO:tm, :]

    def conv_cols(c0, slot):
        cols = slice(c0, c0 + FF_CHUNK)
        ubuf = ubuf_ref.at[slot]
        return (cw_ref[0:1, cols] * ubuf[FFN_HALO - 2:FFN_HALO - 2 + tm, :]
                + cw_ref[1:2, cols] * ubuf[FFN_HALO - 1:FFN_HALO - 1 + tm, :]
                + cw_ref[2:3, cols] * ubuf[FFN_HALO:FFN_HALO + tm, :] + cb_ref[:, cols])

    def up_chunk(j):
        up_cols(j * FF_CHUNK, (2 * j) % FFN_UBUFS)
        up_cols(D_FF + j * FF_CHUNK, (2 * j + 1) % FFN_UBUFS)

    up_chunk(0)
    for j in range(n_chunks):
        if j + 1 < n_chunks:
            up_chunk(j + 1)
        gate = conv_cols(j * FF_CHUNK, (2 * j) % FFN_UBUFS)
        val = conv_cols(D_FF + j * FF_CHUNK, (2 * j + 1) % FFN_UBUFS)
        act = (jax.nn.gelu(gate) * val).astype(BF16)
        part = _dot(act, wdn_ref[j * FF_CHUNK:(j + 1) * FF_CHUNK, :])
        if j == 0:
            acc_ref[...] = part
        else:
            acc_ref[...] += part
    x2 = x1 + _rms(acc_ref[...], gpost_ref[...])

    gate = jax.nn.sigmoid(_dot(_rms(x2, gpe_ref[...]).astype(BF16), wg_ref[...]))
    proj = _dot(p_ref[...].astype(BF16), wp_ref[...])
    xo_ref[...] = x2 + gate * proj


def _post(oa, ob, oc, od, x2d, p2d, wout, gmix, gpre, wup, cw, cb, wdn, gpost, gpe, wg, wp, batch, seq):
    n = x2d.shape[0]
    tm = TM_PROJ
    nt = seq // tm
    rows = lambda width: pl.BlockSpec((tm, width), lambda b, i: (b * nt + i, 0))
    const = lambda shape: pl.BlockSpec(shape, lambda b, i: (0, 0), pipeline_mode=pl.Buffered(1))
    return pl.pallas_call(
        _post_kernel,
        grid=(batch, nt),
        in_specs=[rows(GW), rows(GW), rows(GW), rows(GW), rows(D_MODEL), rows(P_DIM),
                  const((4 * GW, D_MODEL)), const((1, D_MODEL)),
                  const((1, D_MODEL)), const((D_MODEL, 2 * D_FF)), const((3, 2 * D_FF)), const((1, 2 * D_FF)),
                  const((D_FF, D_MODEL)), const((1, D_MODEL)),
                  const((1, D_MODEL)), const((D_MODEL, D_MODEL)), const((P_DIM, D_MODEL))],
        out_specs=rows(D_MODEL),
        out_shape=jax.ShapeDtypeStruct((n, D_MODEL), F32),
        scratch_shapes=[pltpu.VMEM((FFN_HALO, 2 * D_FF), F32),
                        pltpu.VMEM((FFN_UBUFS, FFN_HALO + tm, FF_CHUNK), F32),
                        pltpu.VMEM((tm, D_MODEL), F32)],
        compiler_params=_cparams(2),
        name="post",
    )(oa, ob, oc, od, x2d, p2d, wout, gmix, gpre, wup, cw, cb, wdn, gpost, gpe, wg, wp)


def kernel(x, p, positions, pre_mix_norm, w_in, gmlp_ln_g, gmlp_ln_b, gmlp_ws, gmlp_bs, conv_dw_w, conv_dw_b, conv_ln_g, conv_ln_b, conv_pw_w, conv_pw_b, diff_lq1, diff_lk1, diff_lq2, diff_lk2, diff_subln_g, out_norm_a, out_norm_b, out_norm_c, w_out, post_mix_norm, pre_ffn_norm, w_up, ffn_conv_w, ffn_conv_b, w_down, post_ffn_norm, pe_gate_norm, w_pe_gate, w_pe_proj):
    batch, seq, d = x.shape
    depth = w_in.shape[0]
    n = batch * seq
    row = lambda a: a.reshape(1, -1).astype(F32)

    x2d = x.reshape(n, d)
    tabs = _rope_tables(positions)
    for i in range(depth):
        w_bf16 = w_in[i].astype(BF16)
        group = lambda g: w_bf16[:, g * GW:(g + 1) * GW]
        wn = jnp.concatenate([group(g) for g in ZN_GROUPS], axis=1)
        wt = jnp.concatenate([group(g) for g in ZT_GROUPS], axis=1).T
        zn, zt = _inproj(x2d, row(pre_mix_norm[i]), wn, wt, tabs)
        out_a = _moba(zn, zt, row(out_norm_a[i]), batch, seq)
        lvec = jnp.stack([diff_lq1[i], diff_lk1[i], diff_lq2[i], diff_lk2[i]]).astype(F32)
        lam_init = 0.8 - 0.6 * math.exp(-0.3 * i)
        out_d = _diff(zn, zt, lvec, row(jnp.tile(diff_subln_g[i], N_HEADS)), lam_init, batch, seq)
        bc_prm = (row(gmlp_ln_g[i]), row(gmlp_ln_b[i]),
                  gmlp_ws[i].reshape(N_HEADS * GMLP_CHUNK, GMLP_CHUNK),
                  jnp.repeat(gmlp_bs[i].T, HEAD_DIM, axis=1), row(out_norm_b[i]),
                  conv_dw_w[i], row(conv_dw_b[i]), row(conv_ln_g[i]), row(conv_ln_b[i]),
                  conv_pw_w[i].astype(BF16), row(conv_pw_b[i]), row(out_norm_c[i]))
        out_b, out_c = _gmlp_conv(zn, bc_prm, batch, seq)
        x2d = _post(out_a, out_b, out_c, out_d, x2d, p[i].reshape(n, P_DIM),
                    w_out[i].astype(BF16), row(post_mix_norm[i]),
                    row(pre_ffn_norm[i]), w_up[i].astype(BF16), ffn_conv_w[i], row(ffn_conv_b[i]),
                    w_down[i].astype(BF16), row(post_ffn_norm[i]),
                    row(pe_gate_norm[i]), w_pe_gate[i].astype(BF16), w_pe_proj[i].astype(BF16), batch, seq)
    return x2d.reshape(batch, seq, d)
```
